```python
import jax, jax.numpy as jnp
from jax import lax
import numpy as np

D_MODEL = 1024
BATCH = 8
SEQ = 8192
DEPTH = 1

MEM_LEN = 256
HEAD_DIM = 64
NSA_HEADS = 8
NSA_GROUPS = 2
NSA_HPG = NSA_HEADS // NSA_GROUPS
NSA_WIDTH = NSA_HEADS * HEAD_DIM
KV_WIDTH = NSA_GROUPS * HEAD_DIM
CMP_BLOCK = 32
CMP_STRIDE = 16
CMP_HIDDEN = 256
SLC_BLOCK = 64
SLC_TOPK = 16
WINDOW = 512
NSA_Q_BLOCK = 64
FORCE = 1e6
CONV_WIDTH = 512
CONV_KSIZE = 31
MEM_HEADS = 4
MEM_HEAD_DIM = 128
MEM_WIDTH = MEM_HEADS * MEM_HEAD_DIM
N_BRANCH = 3
D_FF = -(-8 * D_MODEL // (3 * 256)) * 256
IN_COLS = NSA_WIDTH + 6 * KV_WIDTH + 3 * NSA_HEADS + 2 * CONV_WIDTH + MEM_WIDTH + N_BRANCH * D_MODEL
RMS_EPS = 1e-6
LN_EPS = 1e-5

kernel_name = "nsa_conformer_memxattn_gated_hybrid"


def rms_norm(x, g):
    xf = x.astype(jnp.float32)
    y = xf * lax.rsqrt(jnp.mean(xf * xf, axis=-1, keepdims=True) + RMS_EPS)
    return (y * g.astype(jnp.float32)).astype(x.dtype)


def layer_norm(x, g, b):
    xf = x.astype(jnp.float32)
    mu = jnp.mean(xf, axis=-1, keepdims=True)
    var = jnp.mean(jnp.square(xf - mu), axis=-1, keepdims=True)
    y = (xf - mu) * lax.rsqrt(var + LN_EPS) * g.astype(jnp.float32) + b.astype(jnp.float32)
    return y.astype(x.dtype)


def masked_softmax(s, mask):
    s = jnp.where(mask, s.astype(jnp.float32), -jnp.inf)
    m = jnp.max(s, axis=-1, keepdims=True)
    m = jnp.where(jnp.isfinite(m), m, 0.0)
    e = jnp.exp(s - m)
    den = jnp.sum(e, axis=-1, keepdims=True)
    return e / jnp.where(den > 0, den, 1.0)


def compress_blocks(kv, pos, w1, w2):
    b, g, s, dh = kv.shape
    chunks = kv.reshape(b, g, s // CMP_STRIDE, CMP_STRIDE, dh)
    blocks = jnp.concatenate([chunks[:, :, :-1], chunks[:, :, 1:]], axis=3)
    blocks = (blocks + pos).reshape(b, g, blocks.shape[2], CMP_BLOCK * dh)
    return jax.nn.gelu(blocks @ w1) @ w2


def nsa_attention(q, kc, vc, ks, vs, kw, vw, gate):
    b, g, hpg, s, dh = q.shape
    n_c = kc.shape[2]
    n_sel = ks.shape[2]
    topk = min(SLC_TOPK, n_sel)
    ratio = SLC_BLOCK // CMP_STRIDE
    scale = dh ** -0.5
    qb_len = NSA_Q_BLOCK
    cmp_end = jnp.arange(n_c) * CMP_STRIDE + CMP_BLOCK - 1
    sel_ids = jnp.arange(n_sel)
    bi = jnp.arange(b)[:, None, None, None]
    gi = jnp.arange(g)[None, :, None, None]

    def block(qb):
        q0 = qb * qb_len
        t = q0 + jnp.arange(qb_len)
        qq = lax.dynamic_slice_in_dim(q, q0, qb_len, axis=3) * scale
        gg = lax.dynamic_slice_in_dim(gate, q0, qb_len, axis=3)

        s_c = jnp.einsum('bghqd,bgcd->bghqc', qq, kc)
        p_c = masked_softmax(s_c, cmp_end[None, :] <= t[:, None])
        o_c = jnp.einsum('bghqc,bgcd->bghqd', p_c.astype(vc.dtype), vc)

        imp = jnp.sum(p_c, axis=2)
        imp = jnp.pad(imp, ((0, 0), (0, 0), (0, 0), (1, ratio * (n_sel + 1) - n_c - 1)))
        rows = imp.reshape(b, g, qb_len, n_sel + 1, ratio)
        imp_s = jnp.sum(rows[..., :-1, :], axis=-1) + rows[..., 1:, 0]
        cur = t // SLC_BLOCK
        forced = (sel_ids[None, :] == 0) | (sel_ids[None, :] == cur[:, None]) | (sel_ids[None, :] == cur[:, None] - 1)
        future = sel_ids[None, :] * SLC_BLOCK > t[:, None]
        imp_s = jnp.where(forced, FORCE, jnp.where(future, -FORCE, imp_s))
        _, idx = lax.top_k(imp_s, topk)

        k_g = ks[bi, gi, idx]
        v_g = vs[bi, gi, idx]
        tok = idx[..., None] * SLC_BLOCK + jnp.arange(SLC_BLOCK)
        s_s = jnp.einsum('bghqd,bgqkld->bghqkl', qq, k_g).reshape(b, g, hpg, qb_len, topk * SLC_BLOCK)
        m_s = (tok <= t[:, None, None]).reshape(b, g, 1, qb_len, topk * SLC_BLOCK)
        p_s = masked_softmax(s_s, m_s).reshape(b, g, hpg, qb_len, topk, SLC_BLOCK)
        o_s = jnp.einsum('bghqkl,bgqkld->bghqd', p_s.astype(v_g.dtype), v_g)

        kwb = lax.dynamic_slice_in_dim(kw, q0, WINDOW + qb_len, axis=2)
        vwb = lax.dynamic_slice_in_dim(vw, q0, WINDOW + qb_len, axis=2)
        pos = q0 - WINDOW + jnp.arange(WINDOW + qb_len)
        diff = t[:, None] - pos[None, :]
        m_w = (diff >= 0) & (diff < WINDOW) & (pos[None, :] >= 0)
        s_w = jnp.einsum('bghqd,bgkd->bghqk', qq, kwb)
        p_w = masked_softmax(s_w, m_w)
        o_w = jnp.einsum('bghqk,bgkd->bghqd', p_w.astype(vwb.dtype), vwb)

        return gg[..., 0:1] * o_c + gg[..., 1:2] * o_s + gg[..., 2:3] * o_w

    out = lax.map(block, jnp.arange(s // qb_len))
    out = jnp.transpose(out, (1, 0, 4, 2, 3, 5))
    return out.reshape(b, s, g * hpg * dh)


def hybrid_layer(x, mem, norm_mix, w_in, nsa_qk_norm, cmp_pos, cmp_w1, cmp_w2, w_nsa_out,
                 conv_w, conv_b, conv_ln_g, conv_ln_b, w_conv_out, norm_mem, w_mem_kv,
                 mem_qk_norm, w_mem_out, w_out, norm_ffn, w_gate, w_up, w_down):
    b, s, _ = x.shape
    xn = rms_norm(x, norm_mix)
    proj = xn @ w_in
    sizes = [NSA_WIDTH, 6 * KV_WIDTH, 3 * NSA_HEADS, 2 * CONV_WIDTH, MEM_WIDTH, N_BRANCH * D_MODEL]
    points = [int(p) for p in np.cumsum(sizes)[:-1]]
    q_nsa, kv_nsa, g_nsa, conv_in, q_mem, g_merge = jnp.split(proj, points, axis=-1)

    def heads(t_, n):
        return jnp.transpose(t_.reshape(b, s, n, HEAD_DIM), (0, 2, 1, 3))
    q = rms_norm(heads(q_nsa, NSA_HEADS), nsa_qk_norm[0]).reshape(b, NSA_GROUPS, NSA_HPG, s, HEAD_DIM)
    k_c, v_c, k_s, v_s, k_w, v_w = [heads(t_, NSA_GROUPS) for t_ in jnp.split(kv_nsa, 6, axis=-1)]
    kc = rms_norm(compress_blocks(k_c, cmp_pos[0], cmp_w1[0], cmp_w2[0]), nsa_qk_norm[1])
    vc = compress_blocks(v_c, cmp_pos[1], cmp_w1[1], cmp_w2[1])
    n_sel = s // SLC_BLOCK
    ks = rms_norm(k_s, nsa_qk_norm[2]).reshape(b, NSA_GROUPS, n_sel, SLC_BLOCK, HEAD_DIM)
    vs = v_s.reshape(b, NSA_GROUPS, n_sel, SLC_BLOCK, HEAD_DIM)
    pad = ((0, 0), (0, 0), (WINDOW, 0), (0, 0))
    kw = jnp.pad(rms_norm(k_w, nsa_qk_norm[3]), pad)
    vw = jnp.pad(v_w, pad)
    gate = jnp.transpose(jax.nn.sigmoid(g_nsa).reshape(b, s, NSA_GROUPS, NSA_HPG, 3), (0, 2, 3, 1, 4))
    o_nsa = nsa_attention(q, kc, vc, ks, vs, kw, vw, gate)
    br_nsa = o_nsa @ w_nsa_out

    a, ag = jnp.split(conv_in, 2, axis=-1)
    u = a * jax.nn.sigmoid(ag)
    u = jnp.pad(u, ((0, 0), (CONV_KSIZE - 1, 0), (0, 0)))
    u = lax.conv_general_dilated(u, conv_w, (1,), 'VALID', dimension_numbers=('NWC', 'WIO', 'NWC'),
                                 feature_group_count=CONV_WIDTH) + conv_b
    u = jax.nn.silu(layer_norm(u, conv_ln_g, conv_ln_b))
    br_conv = u @ w_conv_out

    qm = rms_norm(q_mem.reshape(b, s, MEM_HEADS, MEM_HEAD_DIM), mem_qk_norm[0])
    kvm = rms_norm(mem, norm_mem) @ w_mem_kv
    km, vm = jnp.split(kvm, 2, axis=-1)
    km = rms_norm(km.reshape(b, -1, MEM_HEADS, MEM_HEAD_DIM), mem_qk_norm[1])
    vm = vm.reshape(b, -1, MEM_HEADS, MEM_HEAD_DIM)
    s_m = jnp.einsum('bshd,bmhd->bhsm', qm, km).astype(jnp.float32) * (MEM_HEAD_DIM ** -0.5)
    p_m = jax.nn.softmax(s_m, axis=-1)
    o_m = jnp.einsum('bhsm,bmhd->bshd', p_m.astype(vm.dtype), vm).reshape(b, s, MEM_WIDTH)
    br_mem = o_m @ w_mem_out

    gm = jax.nn.sigmoid(g_merge).reshape(b, s, N_BRANCH, D_MODEL)
    merged = gm[:, :, 0] * br_nsa + gm[:, :, 1] * br_conv + gm[:, :, 2] * br_mem
    h = x + merged @ w_out

    hn = rms_norm(h, norm_ffn)
    return h + (jax.nn.silu(hn @ w_gate) * (hn @ w_up)) @ w_down


def setup_inputs(seed: int = 0) -> dict:
    key = jax.random.key(seed)
    k = jax.random.split(key, 24)
    f32 = jnp.float32
    L = DEPTH

    def nrm(kk, shape, fan_in):
        return jax.random.normal(kk, shape, f32) * (fan_in ** -0.5)

    def gain(kk, shape):
        return 1.0 + 0.02 * jax.random.normal(kk, shape, f32)

    def small(kk, shape):
        return 0.01 * jax.random.normal(kk, shape, f32)

    return {
        "x": jax.random.normal(k[0], (BATCH, SEQ, D_MODEL), f32),
        "mem": jax.random.normal(k[1], (BATCH, MEM_LEN, D_MODEL), f32),
        "norm_mix": gain(k[2], (L, D_MODEL)),
        "w_in": nrm(k[3], (L, D_MODEL, IN_COLS), D_MODEL),
        "nsa_qk_norm": gain(k[4], (L, 4, HEAD_DIM)),
        "cmp_pos": 2.0 * small(k[5], (L, 2, CMP_BLOCK, HEAD_DIM)),
        "cmp_w1": nrm(k[6], (L, 2, CMP_BLOCK * HEAD_DIM, CMP_HIDDEN), CMP_BLOCK * HEAD_DIM),
        "cmp_w2": nrm(k[7], (L, 2, CMP_HIDDEN, HEAD_DIM), CMP_HIDDEN),
        "w_nsa_out": nrm(k[8], (L, NSA_WIDTH, D_MODEL), NSA_WIDTH),
        "conv_w": nrm(k[9], (L, CONV_KSIZE, 1, CONV_WIDTH), CONV_KSIZE),
        "conv_b": small(k[10], (L, CONV_WIDTH)),
        "conv_ln_g": gain(k[11], (L, CONV_WIDTH)),
        "conv_ln_b": small(k[12], (L, CONV_WIDTH)),
        "w_conv_out": nrm(k[13], (L, CONV_WIDTH, D_MODEL), CONV_WIDTH),
        "norm_mem": gain(k[14], (L, D_MODEL)),
        "w_mem_kv": nrm(k[15], (L, D_MODEL, 2 * MEM_WIDTH), D_MODEL),
        "mem_qk_norm": gain(k[16], (L, 2, MEM_HEAD_DIM)),
        "w_mem_out": nrm(k[17], (L, MEM_WIDTH, D_MODEL), MEM_WIDTH),
        "w_out": nrm(k[18], (L, D_MODEL, D_MODEL), D_MODEL),
        "norm_ffn": gain(k[19], (L, D_MODEL)),
        "w_gate": nrm(k[20], (L, D_MODEL, D_FF), D_MODEL),
        "w_up": nrm(k[21], (L, D_MODEL, D_FF), D_MODEL),
        "w_down": nrm(k[22], (L, D_FF, D_MODEL), D_FF),
    }


def reference(x, mem, norm_mix, w_in, nsa_qk_norm, cmp_pos, cmp_w1, cmp_w2, w_nsa_out,
              conv_w, conv_b, conv_ln_g, conv_ln_b, w_conv_out, norm_mem, w_mem_kv,
              mem_qk_norm, w_mem_out, w_out, norm_ffn, w_gate, w_up, w_down):
    h = x
    for l in range(DEPTH):
        h = hybrid_layer(h, mem, norm_mix[l], w_in[l], nsa_qk_norm[l], cmp_pos[l], cmp_w1[l], cmp_w2[l],
                         w_nsa_out[l], conv_w[l], conv_b[l], conv_ln_g[l], conv_ln_b[l], w_conv_out[l],
                         norm_mem[l], w_mem_kv[l], mem_qk_norm[l], w_mem_out[l], w_out[l], norm_ffn[l],
                         w_gate[l], w_up[l], w_down[l])
    return h
```

```python
import functools

import numpy as np
import jax
import jax.numpy as jnp
from jax import lax
from jax.experimental import pallas as pl
from jax.experimental.pallas import tpu as pltpu

F32 = jnp.float32
BF16 = jnp.bfloat16

D_MODEL = 1024
HEAD_DIM = 64
NSA_HEADS = 8
NSA_GROUPS = 2
NSA_HPG = NSA_HEADS // NSA_GROUPS
NSA_WIDTH = NSA_HEADS * HEAD_DIM
KV_WIDTH = NSA_GROUPS * HEAD_DIM
CMP_BLOCK = 32
CMP_STRIDE = 16
CMP_HIDDEN = 256
SLC_BLOCK = 64
SLC_TOPK = 16
WINDOW = 512
FORCE = 1e6
CONV_WIDTH = 512
CONV_KSIZE = 31
MEM_HEADS = 4
MEM_HEAD_DIM = 128
MEM_WIDTH = MEM_HEADS * MEM_HEAD_DIM
N_BRANCH = 3
D_FF = 2816
RMS_EPS = 1e-6
LN_EPS = 1e-5

LANES = 128
VMEM_LIMIT = 56 * 1024 * 1024

TM_PROJ = 256
TQ = 128
TK = 256
TT_CONV = 512
CONV_HALO = 32
TM_MERGE = 512
TM_FFN = 512
FF_CHUNK = D_FF // 2
MASKED = -1e30
SEL_BIAS = -1e9
GATE_PAD = 16


def _dot(a, b):
    return jnp.dot(a, b, preferred_element_type=F32)


def _dot_nt(a, b):
    return lax.dot_general(a, b, (((1,), (1,)), ((), ())), preferred_element_type=F32)


def _split_dot(x, w):
    hi = x.astype(BF16)
    lo = (x - hi.astype(F32)).astype(BF16)
    return _dot(hi, w) + _dot(lo, w)


def _head_norm(t, bd, gain):
    ms = _split_dot(t * t, bd)
    return t * lax.rsqrt(ms + RMS_EPS) * gain


def _sigmoid(x):
    return 1.0 / (1.0 + jnp.exp(-x))


def _const_spec(shape):
    nd = len(shape)
    return pl.BlockSpec(shape, lambda *_: (0,) * nd, pipeline_mode=pl.Buffered(1))


def _params(*sem):
    return pltpu.CompilerParams(dimension_semantics=sem, vmem_limit_bytes=VMEM_LIMIT)


def _inproj_kernel(x_ref, g_ref, wq_ref, wkv_ref, wg_ref, wc_ref, wm_ref, wgm_ref,
                   bdq_ref, bdk_ref, bdm_ref, gq_ref, gks_ref, gkw_ref, gqm_ref,
                   q_ref, kc_ref, vc_ref, ks_ref, vs_ref, kw_ref, vw_ref, gn_ref, u_ref, qm_ref, gm_ref):
    x = x_ref[...]
    ms = jnp.mean(x * x, axis=-1, keepdims=True)
    xn = (x * lax.rsqrt(ms + RMS_EPS) * g_ref[...]).astype(BF16)

    q_ref[...] = _head_norm(_dot(xn, wq_ref[...]), bdq_ref[...], gq_ref[...]).astype(BF16)

    pkv = _dot(xn, wkv_ref[...])
    kc_ref[...] = pkv[:, 0 * KV_WIDTH:1 * KV_WIDTH]
    vc_ref[...] = pkv[:, 1 * KV_WIDTH:2 * KV_WIDTH]
    ks_ref[...] = _head_norm(pkv[:, 2 * KV_WIDTH:3 * KV_WIDTH], bdk_ref[...], gks_ref[...]).astype(BF16)
    vs_ref[...] = pkv[:, 3 * KV_WIDTH:4 * KV_WIDTH].astype(BF16)
    kw_ref[...] = _head_norm(pkv[:, 4 * KV_WIDTH:5 * KV_WIDTH], bdk_ref[...], gkw_ref[...]).astype(BF16)
    vw_ref[...] = pkv[:, 5 * KV_WIDTH:6 * KV_WIDTH].astype(BF16)

    gn_ref[...] = _sigmoid(_dot(xn, wg_ref[...]))

    pc = _dot(xn, wc_ref[...])
    u_ref[...] = pc[:, :CONV_WIDTH] * _sigmoid(pc[:, CONV_WIDTH:])

    qm_ref[...] = _head_norm(_dot(xn, wm_ref[...]), bdm_ref[...], gqm_ref[...]).astype(BF16)

    gm_ref[...] = _sigmoid(_dot(xn, wgm_ref[...])).astype(BF16)


def _block_diag(n, blk):
    idx = np.arange(n) // blk
    return jnp.asarray((idx[:, None] == idx[None, :]).astype(np.float32) / blk, dtype=BF16)


def _inproj(x2, norm_mix, w_in, nsa_qk_norm, mem_qk_norm):
    m = x2.shape[0]
    tm = min(TM_PROJ, m)
    sizes = [NSA_WIDTH, 6 * KV_WIDTH, 3 * NSA_HEADS, 2 * CONV_WIDTH, MEM_WIDTH, N_BRANCH * D_MODEL]
    pts = np.cumsum(sizes)[:-1]
    wq, wkv, wg, wc, wm, wgm = [w.astype(BF16) for w in jnp.split(w_in, pts, axis=1)]
    wg = jnp.pad(wg, ((0, 0), (0, LANES - 3 * NSA_HEADS)))
    gq = (jnp.tile(nsa_qk_norm[0], NSA_HEADS) * (HEAD_DIM ** -0.5))[None, :]
    gks = jnp.tile(nsa_qk_norm[2], NSA_GROUPS)[None, :]
    gkw = jnp.tile(nsa_qk_norm[3], NSA_GROUPS)[None, :]
    gqm = (jnp.tile(mem_qk_norm[0], MEM_HEADS) * (MEM_HEAD_DIM ** -0.5))[None, :]
    bdq = _block_diag(NSA_WIDTH, HEAD_DIM)
    bdk = _block_diag(KV_WIDTH, HEAD_DIM)
    bdm = _block_diag(MEM_WIDTH, MEM_HEAD_DIM)
    consts = [norm_mix[None, :], wq, wkv, wg, wc, wm, wgm, bdq, bdk, bdm, gq, gks, gkw, gqm]

    def row(n):
        return pl.BlockSpec((tm, n), lambda i: (i, 0))

    outs = [(NSA_WIDTH, BF16), (KV_WIDTH, F32), (KV_WIDTH, F32), (KV_WIDTH, BF16), (KV_WIDTH, BF16),
            (KV_WIDTH, BF16), (KV_WIDTH, BF16), (LANES, F32), (CONV_WIDTH, F32), (MEM_WIDTH, BF16),
            (N_BRANCH * D_MODEL, BF16)]
    return pl.pallas_call(
        _inproj_kernel,
        grid=(m // tm,),
        in_specs=[row(D_MODEL)] + [_const_spec(c.shape) for c in consts],
        out_specs=[row(n) for n, _ in outs],
        out_shape=[jax.ShapeDtypeStruct((m, n), dt) for n, dt in outs],
        compiler_params=_params("parallel"),
        name="inproj",
    )(x2, *consts)


def _gelu_tanh(x):
    return 0.5 * x * (1.0 + jnp.tanh(np.sqrt(2.0 / np.pi) * (x + 0.044715 * (x * x * x))))


def _compress_kernel(x_ref, plo_ref, phi_ref, w1a_ref, w1b_ref, w2_ref, gain_ref, o_ref, *, normalize):
    x = x_ref[0, 0]
    a = _dot((x + plo_ref[...]).astype(BF16), w1a_ref[...])
    b = _dot((x + phi_ref[...]).astype(BF16), w1b_ref[...])
    n = x.shape[0]
    h = a + pltpu.roll(b, n - 1, 0)
    y = _dot(_gelu_tanh(h).astype(BF16), w2_ref[...])
    if normalize:
        ms = jnp.mean(y * y, axis=-1, keepdims=True)
        y = y * lax.rsqrt(ms + RMS_EPS) * gain_ref[...]
    o_ref[0, 0] = y.astype(o_ref.dtype)


def _compress(xc, pos, w1, w2, gain, normalize):
    b, g, n, width = xc.shape
    half = CMP_STRIDE * HEAD_DIM
    plo = pos[:CMP_STRIDE].reshape(1, half)
    phi = pos[CMP_STRIDE:].reshape(1, half)
    w1a = w1[:half].astype(BF16)
    w1b = w1[half:].astype(BF16)
    consts = [plo, phi, w1a, w1b, w2.astype(BF16), gain[None, :]]
    return pl.pallas_call(
        functools.partial(_compress_kernel, normalize=normalize),
        grid=(b, g),
        in_specs=[pl.BlockSpec((1, 1, n, width), lambda i, j: (i, j, 0, 0))] + [_const_spec(c.shape) for c in consts],
        out_specs=pl.BlockSpec((1, 1, n, HEAD_DIM), lambda i, j: (i, j, 0, 0)),
        out_shape=jax.ShapeDtypeStruct((b, g, n, HEAD_DIM), BF16),
        compiler_params=_params("parallel", "parallel"),
        name="compress_k" if normalize else "compress_v",
    )(xc, *consts)


def _nsa_kernel(q_ref, kc_ref, vc_ref, ks_ref, vs_ref, kw_ref, vw_ref, gate_ref, mt_ref, o_ref,
                qaug_scr, m_scr, acc_scr, *, n_cmp, n_sel):
    tq = q_ref.shape[2]
    rows = NSA_HPG * tq
    i = pl.program_id(2)
    q0 = i * tq
    q_st = q_ref[0].reshape(rows, HEAD_DIM)

    def stack(a):
        return jnp.concatenate([a] * NSA_HPG, axis=0)

    t_col = q0 + lax.broadcasted_iota(jnp.int32, (tq, 1), 0)

    cmp_end = lax.broadcasted_iota(jnp.int32, (1, n_cmp), 1) * CMP_STRIDE + (CMP_BLOCK - 1)
    cvalid = stack(cmp_end <= t_col)
    s = jnp.where(cvalid, _dot_nt(q_st, kc_ref[0, 0]), MASKED)
    mx = jnp.max(s, axis=-1, keepdims=True)
    e = jnp.where(cvalid, jnp.exp(s - mx), 0.0)
    den = jnp.sum(e, axis=-1, keepdims=True)
    p = e * (1.0 / jnp.where(den > 0, den, 1.0))
    o_c = _dot(p.astype(BF16), vc_ref[0, 0])
    psum = p[0:tq]
    for h in range(1, NSA_HPG):
        psum = psum + p[h * tq:(h + 1) * tq]

    hi = psum.astype(BF16)
    lo = (psum - hi.astype(F32)).astype(BF16)
    imp = _dot_nt(mt_ref[...], hi) + _dot_nt(mt_ref[...], lo)
    blk = lax.broadcasted_iota(jnp.int32, (n_sel, tq), 0)
    t_row = q0 + lax.broadcasted_iota(jnp.int32, (n_sel, tq), 1)
    cur = t_row // SLC_BLOCK
    forced = (blk == 0) | (blk == cur) | (blk == cur - 1)
    future = blk * SLC_BLOCK > t_row
    v = jnp.where(forced, FORCE, jnp.where(future, -FORCE, imp))
    blk_f = blk.astype(F32)
    for _ in range(min(SLC_TOPK, n_sel)):
        top = jnp.max(v, axis=0, keepdims=True)
        first = jnp.min(jnp.where(v == top, blk_f, float(n_sel)), axis=0, keepdims=True)
        v = jnp.where(blk_f == first, -jnp.inf, v)
    bias_t = jnp.where(future, SEL_BIAS, jnp.where(v == -jnp.inf, 0.0, SEL_BIAS))
    bias = stack(jnp.transpose(bias_t).astype(BF16))

    qaug_scr[:, 0:HEAD_DIM] = q_st
    qaug_scr[:, HEAD_DIM:LANES] = jnp.zeros((rows, LANES - HEAD_DIM), BF16)
    qaug_scr[:, LANES:LANES + n_sel] = bias
    m_scr[...] = jnp.full(m_scr.shape, MASKED, F32)
    acc_scr[...] = jnp.zeros(acc_scr.shape, F32)

    def sel_chunk(c, causal):
        k0 = pl.multiple_of(c * TK, TK)
        sc = _dot_nt(qaug_scr[...], ks_ref[0, 0, pl.ds(k0, TK), :])
        if causal:
            tok = k0 + lax.broadcasted_iota(jnp.int32, (1, TK), 1)
            sc = jnp.where(stack(tok <= t_col), sc, MASKED)
        m_old = m_scr[...]
        m_new = jnp.maximum(m_old, jnp.max(sc, axis=-1, keepdims=True))
        pc = jnp.exp(sc - jnp.concatenate([m_new] * (TK // LANES), axis=1))
        acc_scr[...] = jnp.exp(m_old - m_new) * acc_scr[...] + _dot(pc.astype(BF16), vs_ref[0, 0, pl.ds(k0, TK), :])
        m_scr[...] = m_new

    last = q0 // TK

    def body(c, carry):
        sel_chunk(c, False)
        return carry

    lax.fori_loop(0, last, body, 0)
    sel_chunk(last, True)
    acc = acc_scr[...]
    o_s = acc[:, :HEAD_DIM] * (1.0 / acc[:, HEAD_DIM:HEAD_DIM + 1])

    span = WINDOW + tq
    start = pl.multiple_of(q0, tq)
    pos = q0 - WINDOW + lax.broadcasted_iota(jnp.int32, (1, span), 1)
    diff = t_col - pos
    wvalid = stack((diff >= 0) & (diff < WINDOW) & (pos >= 0))
    sw = jnp.where(wvalid, _dot_nt(q_st, kw_ref[0, 0, pl.ds(start, span), :]), MASKED)
    pw = jnp.exp(sw - jnp.max(sw, axis=-1, keepdims=True))
    ow = _dot(pw.astype(BF16), vw_ref[0, 0, pl.ds(start, span), :])
    o_w = ow[:, :HEAD_DIM] * (1.0 / ow[:, HEAD_DIM:HEAD_DIM + 1])

    g = gate_ref[0, 0]
    for h in range(NSA_HPG):
        r = slice(h * tq, (h + 1) * tq)
        out = (g[:, 3 * h:3 * h + 1] * o_c[r] + g[:, 3 * h + 1:3 * h + 2] * o_s[r]
               + g[:, 3 * h + 2:3 * h + 3] * o_w[r])
        o_ref[0, :, h * HEAD_DIM:(h + 1) * HEAD_DIM] = out.astype(o_ref.dtype)


def _importance_matrix(n_cmp, n_sel):
    ratio = SLC_BLOCK // CMP_STRIDE
    c = np.arange(n_cmp)[None, :]
    j = np.arange(n_sel)[:, None]
    return jnp.asarray(((c >= ratio * j - 1) & (c <= ratio * j + ratio - 1)).astype(np.float32), dtype=BF16)


def _nsa(q, kc, vc, ks, vs, kw, vw, gate):
    b, _, s, _ = q.shape
    tq = min(TQ, s)
    n_cmp = s // CMP_STRIDE
    n_sel = s // SLC_BLOCK
    assert n_sel <= LANES and s % TK == 0
    blk_id = jnp.arange(s) // SLC_BLOCK
    onehot = (blk_id[:, None] == jnp.arange(n_sel)[None, :]).astype(BF16)
    onehot = jnp.broadcast_to(onehot, (b, NSA_GROUPS, s, n_sel))
    k_aug = jnp.concatenate([ks, jnp.zeros((b, NSA_GROUPS, s, LANES - HEAD_DIM), BF16), onehot], axis=-1)
    ones = jnp.ones((b, NSA_GROUPS, s, 1), BF16)
    vpad = jnp.zeros((b, NSA_GROUPS, s, LANES - HEAD_DIM - 1), BF16)
    vs_aug = jnp.concatenate([vs, ones, vpad], axis=-1)
    vw_aug = jnp.concatenate([vw, ones, vpad], axis=-1)
    lpad = ((0, 0), (0, 0), (WINDOW, 0), (0, 0))
    kw_p = jnp.pad(kw, lpad)
    vw_p = jnp.pad(vw_aug, lpad)
    mt = _importance_matrix(n_cmp, n_sel)
    rows = NSA_HPG * tq

    def per_group(shape):
        return pl.BlockSpec((1, 1) + shape, lambda bi, gi, i: (bi, gi, 0, 0))

    return pl.pallas_call(
        functools.partial(_nsa_kernel, n_cmp=n_cmp, n_sel=n_sel),
        grid=(b, NSA_GROUPS, s // tq),
        in_specs=[
            pl.BlockSpec((1, NSA_HPG, tq, HEAD_DIM), lambda bi, gi, i: (bi, gi, i, 0)),
            per_group((n_cmp, HEAD_DIM)), per_group((n_cmp, HEAD_DIM)),
            per_group((s, LANES + n_sel)), per_group((s, LANES)),
            per_group((s + WINDOW, HEAD_DIM)), per_group((s + WINDOW, LANES)),
            pl.BlockSpec((1, 1, tq, GATE_PAD), lambda bi, gi, i: (bi, gi, i, 0)),
            _const_spec(mt.shape),
        ],
        out_specs=pl.BlockSpec((1, tq, NSA_HPG * HEAD_DIM), lambda bi, gi, i: (bi, i, gi)),
        out_shape=jax.ShapeDtypeStruct((b, s, NSA_WIDTH), BF16),
        scratch_shapes=[pltpu.VMEM((rows, LANES + n_sel), BF16),
                        pltpu.VMEM((rows, LANES), F32),
                        pltpu.VMEM((rows, LANES), F32)],
        compiler_params=_params("parallel", "parallel", "arbitrary"),
        name="nsa",
    )(q, kc, vc, k_aug, vs_aug, kw_p, vw_p, gate, mt)


def _conv_kernel(prev_ref, cur_ref, w_ref, b_ref, g_ref, beta_ref, o_ref, buf):
    tt = cur_ref.shape[1]
    i = pl.program_id(1)
    hist = prev_ref[0]
    buf[0:CONV_HALO] = jnp.where(i > 0, hist, jnp.zeros_like(hist))
    buf[CONV_HALO:] = cur_ref[0]
    off = CONV_HALO - (CONV_KSIZE - 1)
    acc = jnp.zeros((tt, CONV_WIDTH), F32) + b_ref[...]
    for k in range(CONV_KSIZE):
        acc = acc + buf[off + k:off + k + tt] * w_ref[k:k + 1]
    mu = jnp.mean(acc, axis=-1, keepdims=True)
    d = acc - mu
    var = jnp.mean(d * d, axis=-1, keepdims=True)
    y = d * lax.rsqrt(var + LN_EPS) * g_ref[...] + beta_ref[...]
    o_ref[0] = (y * _sigmoid(y)).astype(o_ref.dtype)


def _conv_module(u, conv_w, conv_b, ln_g, ln_b):
    b, s, _ = u.shape
    tt = min(TT_CONV, s)
    per = tt // CONV_HALO
    consts = [conv_w.reshape(CONV_KSIZE, CONV_WIDTH), conv_b[None, :], ln_g[None, :], ln_b[None, :]]
    return pl.pallas_call(
        _conv_kernel,
        grid=(b, s // tt),
        in_specs=[pl.BlockSpec((1, CONV_HALO, CONV_WIDTH), lambda bi, i: (bi, jnp.maximum(i * per - 1, 0), 0)),
                  pl.BlockSpec((1, tt, CONV_WIDTH), lambda bi, i: (bi, i, 0))]
                 + [_const_spec(c.shape) for c in consts],
        out_specs=pl.BlockSpec((1, tt, CONV_WIDTH), lambda bi, i: (bi, i, 0)),
        out_shape=jax.ShapeDtypeStruct((b, s, CONV_WIDTH), BF16),
        scratch_shapes=[pltpu.VMEM((CONV_HALO + tt, CONV_WIDTH), F32)],
        compiler_params=_params("parallel", "parallel"),
        name="conv",
    )(u, u, *consts)


def _memkv_kernel(mem_ref, g_ref, w_ref, bd_ref, gk_ref, km_ref, vm_ref):
    x = mem_ref[0]
    ms = jnp.mean(x * x, axis=-1, keepdims=True)
    xn = (x * lax.rsqrt(ms + RMS_EPS) * g_ref[...]).astype(BF16)
    kv = _dot(xn, w_ref[...])
    km_ref[0] = _head_norm(kv[:, :MEM_WIDTH], bd_ref[...], gk_ref[...]).astype(BF16)
    vm_ref[0] = kv[:, MEM_WIDTH:].astype(BF16)


def _memkv(mem, norm_mem, w_mem_kv, gain_k):
    b, n, _ = mem.shape
    consts = [norm_mem[None, :], w_mem_kv.astype(BF16), _block_diag(MEM_WIDTH, MEM_HEAD_DIM),
              jnp.tile(gain_k, MEM_HEADS)[None, :]]
    spec = pl.BlockSpec((1, n, MEM_WIDTH), lambda i: (i, 0, 0))
    return pl.pallas_call(
        _memkv_kernel,
        grid=(b,),
        in_specs=[pl.BlockSpec((1, n, D_MODEL), lambda i: (i, 0, 0))] + [_const_spec(c.shape) for c in consts],
        out_specs=[spec, spec],
        out_shape=[jax.ShapeDtypeStruct((b, n, MEM_WIDTH), BF16)] * 2,
        compiler_params=_params("parallel"),
        name="memkv",
    )(mem, *consts)


def _merge_kernel(x_ref, onsa_ref, uconv_ref, qm_ref, gm_ref, km_ref, vm_ref,
                  wn_ref, wc_ref, wmo_ref, wo_ref, h_ref):
    qm = qm_ref[...]
    km = km_ref[0]
    vm = vm_ref[0]
    heads = []
    for h in range(MEM_HEADS):
        c = slice(h * MEM_HEAD_DIM, (h + 1) * MEM_HEAD_DIM)
        s = _dot_nt(qm[:, c], km[:, c])
        e = jnp.exp(s - jnp.max(s, axis=-1, keepdims=True))
        p = e * (1.0 / jnp.sum(e, axis=-1, keepdims=True))
        heads.append(_dot(p.astype(BF16), vm[:, c]))
    o_m = jnp.concatenate(heads, axis=1).astype(BF16)
    merged = (gm_ref[:, 0:D_MODEL].astype(F32) * _dot(onsa_ref[...], wn_ref[...])
              + gm_ref[:, D_MODEL:2 * D_MODEL].astype(F32) * _dot(uconv_ref[...], wc_ref[...])
              + gm_ref[:, 2 * D_MODEL:3 * D_MODEL].astype(F32) * _dot(o_m, wmo_ref[...]))
    h_ref[...] = x_ref[...] + _dot(merged.astype(BF16), wo_ref[...])


def _merge(x2, o_nsa, u_conv, qm, gm, km, vm, w_nsa_out, w_conv_out, w_mem_out, w_out, seq):
    m = x2.shape[0]
    tm = min(TM_MERGE, seq)
    per_b = seq // tm
    n_mem = km.shape[1]
    consts = [w_nsa_out.astype(BF16), w_conv_out.astype(BF16), w_mem_out.astype(BF16), w_out.astype(BF16)]

    def row(n):
        return pl.BlockSpec((tm, n), lambda i: (i, 0))

    mem_spec = pl.BlockSpec((1, n_mem, MEM_WIDTH), lambda i: (i // per_b, 0, 0))
    return pl.pallas_call(
        _merge_kernel,
        grid=(m // tm,),
        in_specs=[row(D_MODEL), row(NSA_WIDTH), row(CONV_WIDTH), row(MEM_WIDTH), row(N_BRANCH * D_MODEL),
                  mem_spec, mem_spec] + [_const_spec(c.shape) for c in consts],
        out_specs=row(D_MODEL),
        out_shape=jax.ShapeDtypeStruct((m, D_MODEL), F32),
        compiler_params=_params("parallel"),
        name="merge",
    )(x2, o_nsa, u_conv, qm, gm, km, vm, *consts)


def _ffn_kernel(h_ref, g_ref, wg_ref, wu_ref, wd_ref, o_ref):
    h = h_ref[...]
    ms = jnp.mean(h * h, axis=-1, keepdims=True)
    hn = (h * lax.rsqrt(ms + RMS_EPS) * g_ref[...]).astype(BF16)
    acc = h
    for c in range(0, D_FF, FF_CHUNK):
        gate = _dot(hn, wg_ref[:, c:c + FF_CHUNK])
        up = _dot(hn, wu_ref[:, c:c + FF_CHUNK])
        act = (gate * _sigmoid(gate) * up).astype(BF16)
        acc = acc + _dot(act, wd_ref[c:c + FF_CHUNK, :])
    o_ref[...] = acc


def _ffn(h, norm_ffn, w_gate, w_up, w_down):
    m = h.shape[0]
    tm = min(TM_FFN, m)
    consts = [norm_ffn[None, :], w_gate.astype(BF16), w_up.astype(BF16), w_down.astype(BF16)]
    row = pl.BlockSpec((tm, D_MODEL), lambda i: (i, 0))
    return pl.pallas_call(
        _ffn_kernel,
        grid=(m // tm,),
        in_specs=[row] + [_const_spec(c.shape) for c in consts],
        out_specs=row,
        out_shape=jax.ShapeDtypeStruct((m, D_MODEL), F32),
        compiler_params=_params("parallel"),
        name="ffn",
    )(h, *consts)


def _layer(x, mem, norm_mix, w_in, nsa_qk_norm, cmp_pos, cmp_w1, cmp_w2, w_nsa_out,
           conv_w, conv_b, conv_ln_g, conv_ln_b, w_conv_out, norm_mem, w_mem_kv,
           mem_qk_norm, w_mem_out, w_out, norm_ffn, w_gate, w_up, w_down):
    b, s, _ = x.shape
    x2 = x.reshape(b * s, D_MODEL)
    q, kc_raw, vc_raw, ks, vs, kw, vw, gn, u, qm, gm = _inproj(x2, norm_mix, w_in, nsa_qk_norm, mem_qk_norm)

    def group_major(t):
        return jnp.transpose(t.reshape(b, s, NSA_GROUPS, HEAD_DIM), (0, 2, 1, 3))

    def chunked(t):
        return group_major(t).reshape(b, NSA_GROUPS, s // CMP_STRIDE, CMP_STRIDE * HEAD_DIM)

    kc = _compress(chunked(kc_raw), cmp_pos[0], cmp_w1[0], cmp_w2[0], nsa_qk_norm[1], True)
    vc = _compress(chunked(vc_raw), cmp_pos[1], cmp_w1[1], cmp_w2[1], nsa_qk_norm[1], False)
    q_h = jnp.transpose(q.reshape(b, s, NSA_HEADS, HEAD_DIM), (0, 2, 1, 3))
    gate = gn[:, :3 * NSA_HEADS].reshape(b, s, NSA_GROUPS, 3 * NSA_HPG)
    gate = jnp.pad(jnp.transpose(gate, (0, 2, 1, 3)), ((0, 0), (0, 0), (0, 0), (0, GATE_PAD - 3 * NSA_HPG)))
    o_nsa = _nsa(q_h, kc, vc, group_major(ks), group_major(vs), group_major(kw), group_major(vw), gate)

    u_conv = _conv_module(u.reshape(b, s, CONV_WIDTH), conv_w, conv_b, conv_ln_g, conv_ln_b)
    km, vm = _memkv(mem, norm_mem, w_mem_kv, mem_qk_norm[1])
    h = _merge(x2, o_nsa.reshape(b * s, NSA_WIDTH), u_conv.reshape(b * s, CONV_WIDTH), qm, gm, km, vm,
               w_nsa_out, w_conv_out, w_mem_out, w_out, s)
    out = _ffn(h, norm_ffn, w_gate, w_up, w_down)
    return out.reshape(b, s, D_MODEL)


def kernel(x, mem, norm_mix, w_in, nsa_qk_norm, cmp_pos, cmp_w1, cmp_w2, w_nsa_out, conv_w, conv_b, conv_ln_g, conv_ln_b, w_conv_out, norm_mem, w_mem_kv, mem_qk_norm, w_mem_out, w_out, norm_ffn, w_gate, w_up, w_down):
    h = x
    for l in range(norm_mix.shape[0]):
        h = _layer(h, mem, norm_mix[l], w_in[l], nsa_qk_norm[l], cmp_pos[l], cmp_w1[l], cmp_w2[l],
                   w_nsa_out[l], conv_w[l], conv_b[l], conv_ln_g[l], conv_ln_b[l], w_conv_out[l],
                   norm_mem[l], w_mem_kv[l], mem_qk_norm[l], w_mem_out[l], w_out[l], norm_ffn[l],
                   w_gate[l], w_up[l], w_down[l])
    return h
```

```python
import functools

import numpy as np
import jax
import jax.numpy as jnp
from jax import lax
from jax.experimental import pallas as pl
from jax.experimental.pallas import tpu as pltpu

F32 = jnp.float32
BF16 = jnp.bfloat16

D_MODEL = 1024
HEAD_DIM = 64
NSA_HEADS = 8
NSA_GROUPS = 2
NSA_HPG = NSA_HEADS // NSA_GROUPS
NSA_WIDTH = NSA_HEADS * HEAD_DIM
KV_WIDTH = NSA_GROUPS * HEAD_DIM
CMP_BLOCK = 32
CMP_STRIDE = 16
CMP_HIDDEN = 256
SLC_BLOCK = 64
SLC_TOPK = 16
WINDOW = 512
FORCE = 1e6
CONV_WIDTH = 512
CONV_KSIZE = 31
MEM_HEADS = 4
MEM_HEAD_DIM = 128
MEM_WIDTH = MEM_HEADS * MEM_HEAD_DIM
N_BRANCH = 3
D_FF = 2816
RMS_EPS = 1e-6
LN_EPS = 1e-5

LANES = 128
VMEM_LIMIT = 56 * 1024 * 1024

TM_PROJ = 256
TQ = 256
TK = 512
NSA_CHAINS = 2
LOG2E = 1.4426950408889634
TT_CONV = 512
CONV_HALO = 32
TM_MERGE = 512
TM_FFN = 512
FF_CHUNK = D_FF // 2
MASKED = -1e30
SEL_BIAS = -1e9
GATE_PAD = 16


def _dot(a, b):
    return jnp.dot(a, b, preferred_element_type=F32)


def _dot_nt(a, b):
    return lax.dot_general(a, b, (((1,), (1,)), ((), ())), preferred_element_type=F32)


def _split_dot(x, w):
    hi = x.astype(BF16)
    lo = (x - hi.astype(F32)).astype(BF16)
    return _dot(hi, w) + _dot(lo, w)


def _head_norm(t, bd, gain):
    ms = _split_dot(t * t, bd)
    return t * lax.rsqrt(ms + RMS_EPS) * gain


def _sigmoid(x):
    return 1.0 / (1.0 + jnp.exp(-x))


def _const_spec(shape):
    nd = len(shape)
    return pl.BlockSpec(shape, lambda *_: (0,) * nd, pipeline_mode=pl.Buffered(1))


def _params(*sem):
    return pltpu.CompilerParams(dimension_semantics=sem, vmem_limit_bytes=VMEM_LIMIT)


def _inproj_kernel(x_ref, g_ref, wq_ref, wkv_ref, wg_ref, wc_ref, wm_ref, wgm_ref,
                   bdq_ref, bdk_ref, bdm_ref, gq_ref, gks_ref, gkw_ref, gqm_ref,
                   q_ref, kc_ref, vc_ref, ks_ref, vs_ref, kw_ref, vw_ref, gn_ref, u_ref, qm_ref, gm_ref):
    x = x_ref[...]
    ms = jnp.mean(x * x, axis=-1, keepdims=True)
    xn = (x * lax.rsqrt(ms + RMS_EPS) * g_ref[...]).astype(BF16)

    q_ref[...] = _head_norm(_dot(xn, wq_ref[...]), bdq_ref[...], gq_ref[...]).astype(BF16)

    pkv = _dot(xn, wkv_ref[...])
    kc_ref[...] = pkv[:, 0 * KV_WIDTH:1 * KV_WIDTH]
    vc_ref[...] = pkv[:, 1 * KV_WIDTH:2 * KV_WIDTH]
    ks_ref[...] = _head_norm(pkv[:, 2 * KV_WIDTH:3 * KV_WIDTH], bdk_ref[...], gks_ref[...]).astype(BF16)
    vs_ref[...] = pkv[:, 3 * KV_WIDTH:4 * KV_WIDTH].astype(BF16)
    kw_ref[...] = _head_norm(pkv[:, 4 * KV_WIDTH:5 * KV_WIDTH], bdk_ref[...], gkw_ref[...]).astype(BF16)
    vw_ref[...] = pkv[:, 5 * KV_WIDTH:6 * KV_WIDTH].astype(BF16)

    gn_ref[...] = _sigmoid(_dot(xn, wg_ref[...]))

    pc = _dot(xn, wc_ref[...])
    u_ref[...] = pc[:, :CONV_WIDTH] * _sigmoid(pc[:, CONV_WIDTH:])

    qm_ref[...] = _head_norm(_dot(xn, wm_ref[...]), bdm_ref[...], gqm_ref[...]).astype(BF16)

    gm_ref[...] = _sigmoid(_dot(xn, wgm_ref[...])).astype(BF16)


def _block_diag(n, blk):
    idx = np.arange(n) // blk
    return jnp.asarray((idx[:, None] == idx[None, :]).astype(np.float32) / blk, dtype=BF16)


def _inproj(x2, norm_mix, w_in, nsa_qk_norm, mem_qk_norm):
    m = x2.shape[0]
    tm = min(TM_PROJ, m)
    sizes = [NSA_WIDTH, 6 * KV_WIDTH, 3 * NSA_HEADS, 2 * CONV_WIDTH, MEM_WIDTH, N_BRANCH * D_MODEL]
    pts = np.cumsum(sizes)[:-1]
    wq, wkv, wg, wc, wm, wgm = [w.astype(BF16) for w in jnp.split(w_in, pts, axis=1)]
    wg = jnp.pad(wg, ((0, 0), (0, LANES - 3 * NSA_HEADS)))
    gq = (jnp.tile(nsa_qk_norm[0], NSA_HEADS) * (HEAD_DIM ** -0.5 * LOG2E))[None, :]
    gks = jnp.tile(nsa_qk_norm[2], NSA_GROUPS)[None, :]
    gkw = jnp.tile(nsa_qk_norm[3], NSA_GROUPS)[None, :]
    gqm = (jnp.tile(mem_qk_norm[0], MEM_HEADS) * (MEM_HEAD_DIM ** -0.5))[None, :]
    bdq = _block_diag(NSA_WIDTH, HEAD_DIM)
    bdk = _block_diag(KV_WIDTH, HEAD_DIM)
    bdm = _block_diag(MEM_WIDTH, MEM_HEAD_DIM)
    consts = [norm_mix[None, :], wq, wkv, wg, wc, wm, wgm, bdq, bdk, bdm, gq, gks, gkw, gqm]

    def row(n):
        return pl.BlockSpec((tm, n), lambda i: (i, 0))

    outs = [(NSA_WIDTH, BF16), (KV_WIDTH, F32), (KV_WIDTH, F32), (KV_WIDTH, BF16), (KV_WIDTH, BF16),
            (KV_WIDTH, BF16), (KV_WIDTH, BF16), (LANES, F32), (CONV_WIDTH, F32), (MEM_WIDTH, BF16),
            (N_BRANCH * D_MODEL, BF16)]
    return pl.pallas_call(
        _inproj_kernel,
        grid=(m // tm,),
        in_specs=[row(D_MODEL)] + [_const_spec(c.shape) for c in consts],
        out_specs=[row(n) for n, _ in outs],
        out_shape=[jax.ShapeDtypeStruct((m, n), dt) for n, dt in outs],
        compiler_params=_params("parallel"),
        name="inproj",
    )(x2, *consts)


def _gelu_tanh(x):
    return 0.5 * x * (1.0 + jnp.tanh(np.sqrt(2.0 / np.pi) * (x + 0.044715 * (x * x * x))))


def _compress_kernel(x_ref, plo_ref, phi_ref, w1a_ref, w1b_ref, w2_ref, gain_ref, o_ref, *, normalize):
    x = x_ref[0, 0]
    a = _dot((x + plo_ref[...]).astype(BF16), w1a_ref[...])
    b = _dot((x + phi_ref[...]).astype(BF16), w1b_ref[...])
    n = x.shape[0]
    h = a + pltpu.roll(b, n - 1, 0)
    y = _dot(_gelu_tanh(h).astype(BF16), w2_ref[...])
    if normalize:
        ms = jnp.mean(y * y, axis=-1, keepdims=True)
        y = y * lax.rsqrt(ms + RMS_EPS) * gain_ref[...]
    o_ref[0, 0] = y.astype(o_ref.dtype)


def _compress(xc, pos, w1, w2, gain, normalize):
    b, g, n, width = xc.shape
    half = CMP_STRIDE * HEAD_DIM
    plo = pos[:CMP_STRIDE].reshape(1, half)
    phi = pos[CMP_STRIDE:].reshape(1, half)
    w1a = w1[:half].astype(BF16)
    w1b = w1[half:].astype(BF16)
    consts = [plo, phi, w1a, w1b, w2.astype(BF16), gain[None, :]]
    return pl.pallas_call(
        functools.partial(_compress_kernel, normalize=normalize),
        grid=(b, g),
        in_specs=[pl.BlockSpec((1, 1, n, width), lambda i, j: (i, j, 0, 0))] + [_const_spec(c.shape) for c in consts],
        out_specs=pl.BlockSpec((1, 1, n, HEAD_DIM), lambda i, j: (i, j, 0, 0)),
        out_shape=jax.ShapeDtypeStruct((b, g, n, HEAD_DIM), BF16),
        compiler_params=_params("parallel", "parallel"),
        name="compress_k" if normalize else "compress_v",
    )(xc, *consts)


def _nsa_kernel(q_ref, kc_ref, vc_ref, ks_ref, vs_ref, kw_ref, vw_ref, gate_ref, mt_ref, o_ref,
                qaug_scr, m_scr, acc_scr, *, n_cmp, n_sel):
    tq = q_ref.shape[2]
    hpc = NSA_HPG // NSA_CHAINS
    crow = hpc * tq
    chains = range(NSA_CHAINS)
    i = pl.program_id(2)
    q0 = i * tq

    def q_chain(r):
        return q_ref[0, r * hpc:(r + 1) * hpc].reshape(crow, HEAD_DIM)

    def rows_of(r):
        return slice(r * crow, (r + 1) * crow)

    def stack(a):
        return jnp.concatenate([a] * hpc, axis=0)

    def normalised(o):
        return o[:, :HEAD_DIM] * (1.0 / o[:, HEAD_DIM:HEAD_DIM + 1])

    t_col = q0 + lax.broadcasted_iota(jnp.int32, (tq, 1), 0)

    cmp_end = lax.broadcasted_iota(jnp.int32, (1, n_cmp), 1) * CMP_STRIDE + (CMP_BLOCK - 1)
    cvalid = stack(cmp_end <= t_col)
    o_c = []
    psum = None
    for r in chains:
        s = jnp.where(cvalid, _dot_nt(q_chain(r), kc_ref[0, 0]), MASKED)
        e = jnp.where(cvalid, jnp.exp2(s - jnp.max(s, axis=-1, keepdims=True)), 0.0)
        den = jnp.sum(e, axis=-1, keepdims=True)
        p = e * (1.0 / jnp.where(den > 0, den, 1.0))
        o_c.append(_dot(p.astype(BF16), vc_ref[0, 0]))
        for h in range(hpc):
            ph = p[h * tq:(h + 1) * tq]
            psum = ph if psum is None else psum + ph

    span = WINDOW + tq
    start = pl.multiple_of(q0, tq)
    pos = q0 - WINDOW + lax.broadcasted_iota(jnp.int32, (1, span), 1)
    diff = t_col - pos
    wvalid = stack((diff >= 0) & (diff < WINDOW) & (pos >= 0))
    o_w = []
    for r in chains:
        sw = jnp.where(wvalid, _dot_nt(q_chain(r), kw_ref[0, 0, pl.ds(start, span), :]), MASKED)
        pw = jnp.exp2(sw - jnp.max(sw, axis=-1, keepdims=True))
        o_w.append(normalised(_dot(pw.astype(BF16), vw_ref[0, 0, pl.ds(start, span), :])))

    hi = psum.astype(BF16)
    lo = (psum - hi.astype(F32)).astype(BF16)
    imp = _dot_nt(mt_ref[...], hi) + _dot_nt(mt_ref[...], lo)
    blk = lax.broadcasted_iota(jnp.int32, (n_sel, tq), 0)
    t_row = q0 + lax.broadcasted_iota(jnp.int32, (n_sel, tq), 1)
    cur = t_row // SLC_BLOCK
    forced = (blk == 0) | (blk == cur) | (blk == cur - 1)
    future = blk * SLC_BLOCK > t_row
    v = jnp.where(forced, FORCE, jnp.where(future, -FORCE, imp))
    blk_f = blk.astype(F32)
    for _ in range(min(SLC_TOPK, n_sel)):
        top = jnp.max(v, axis=0, keepdims=True)
        first = jnp.min(jnp.where(v == top, blk_f, float(n_sel)), axis=0, keepdims=True)
        v = jnp.where(blk_f == first, -jnp.inf, v)
    bias_t = jnp.where(future, SEL_BIAS, jnp.where(v == -jnp.inf, 0.0, SEL_BIAS))
    bias = stack(jnp.transpose(bias_t).astype(BF16))

    for r in chains:
        qaug_scr[rows_of(r), 0:HEAD_DIM] = q_chain(r)
        qaug_scr[rows_of(r), HEAD_DIM:LANES] = jnp.zeros((crow, LANES - HEAD_DIM), BF16)
        qaug_scr[rows_of(r), LANES:LANES + n_sel] = bias
    m_scr[...] = jnp.full(m_scr.shape, MASKED, F32)
    acc_scr[...] = jnp.zeros(acc_scr.shape, F32)

    def sel_chunk(c, causal):
        k0 = pl.multiple_of(c * TK, TK)
        if causal:
            tok = k0 + lax.broadcasted_iota(jnp.int32, (1, TK), 1)
            visible = stack(tok <= t_col)
        for r in chains:
            sc = _dot_nt(qaug_scr[rows_of(r)], ks_ref[0, 0, pl.ds(k0, TK), :])
            if causal:
                sc = jnp.where(visible, sc, MASKED)
            m_old = m_scr[rows_of(r)]
            m_new = jnp.maximum(m_old, jnp.max(sc, axis=-1, keepdims=True))
            pc = jnp.exp2(sc - jnp.concatenate([m_new] * (TK // LANES), axis=1))
            acc_scr[rows_of(r)] = (jnp.exp2(m_old - m_new) * acc_scr[rows_of(r)]
                                   + _dot(pc.astype(BF16), vs_ref[0, 0, pl.ds(k0, TK), :]))
            m_scr[rows_of(r)] = m_new

    last = q0 // TK

    def body(c, carry):
        sel_chunk(c, False)
        return carry

    lax.fori_loop(0, last, body, 0)
    sel_chunk(last, True)

    g = gate_ref[0, 0]
    for r in chains:
        o_s = normalised(acc_scr[rows_of(r)])
        for hh in range(hpc):
            h = r * hpc + hh
            rs = slice(hh * tq, (hh + 1) * tq)
            out = (g[:, 3 * h:3 * h + 1] * o_c[r][rs] + g[:, 3 * h + 1:3 * h + 2] * o_s[rs]
                   + g[:, 3 * h + 2:3 * h + 3] * o_w[r][rs])
            o_ref[0, :, h * HEAD_DIM:(h + 1) * HEAD_DIM] = out.astype(o_ref.dtype)


def _importance_matrix(n_cmp, n_sel):
    ratio = SLC_BLOCK // CMP_STRIDE
    c = np.arange(n_cmp)[None, :]
    j = np.arange(n_sel)[:, None]
    return jnp.asarray(((c >= ratio * j - 1) & (c <= ratio * j + ratio - 1)).astype(np.float32), dtype=BF16)


def _nsa(q, kc, vc, ks, vs, kw, vw, gate):
    b, _, s, _ = q.shape
    tq = min(TQ, s)
    n_cmp = s // CMP_STRIDE
    n_sel = s // SLC_BLOCK
    assert n_sel <= LANES and s % TK == 0
    blk_id = jnp.arange(s) // SLC_BLOCK
    onehot = (blk_id[:, None] == jnp.arange(n_sel)[None, :]).astype(BF16)
    onehot = jnp.broadcast_to(onehot, (b, NSA_GROUPS, s, n_sel))
    k_aug = jnp.concatenate([ks, jnp.zeros((b, NSA_GROUPS, s, LANES - HEAD_DIM), BF16), onehot], axis=-1)
    ones = jnp.ones((b, NSA_GROUPS, s, 1), BF16)
    vpad = jnp.zeros((b, NSA_GROUPS, s, LANES - HEAD_DIM - 1), BF16)
    vs_aug = jnp.concatenate([vs, ones, vpad], axis=-1)
    vw_aug = jnp.concatenate([vw, ones, vpad], axis=-1)
    lpad = ((0, 0), (0, 0), (WINDOW, 0), (0, 0))
    kw_p = jnp.pad(kw, lpad)
    vw_p = jnp.pad(vw_aug, lpad)
    mt = _importance_matrix(n_cmp, n_sel)
    rows = NSA_HPG * tq

    def per_group(shape):
        return pl.BlockSpec((1, 1) + shape, lambda bi, gi, i: (bi, gi, 0, 0))

    return pl.pallas_call(
        functools.partial(_nsa_kernel, n_cmp=n_cmp, n_sel=n_sel),
        grid=(b, NSA_GROUPS, s // tq),
        in_specs=[
            pl.BlockSpec((1, NSA_HPG, tq, HEAD_DIM), lambda bi, gi, i: (bi, gi, i, 0)),
            per_group((n_cmp, HEAD_DIM)), per_group((n_cmp, HEAD_DIM)),
            per_group((s, LANES + n_sel)), per_group((s, LANES)),
            per_group((s + WINDOW, HEAD_DIM)), per_group((s + WINDOW, LANES)),
            pl.BlockSpec((1, 1, tq, GATE_PAD), lambda bi, gi, i: (bi, gi, i, 0)),
            _const_spec(mt.shape),
        ],
        out_specs=pl.BlockSpec((1, tq, NSA_HPG * HEAD_DIM), lambda bi, gi, i: (bi, i, gi)),
        out_shape=jax.ShapeDtypeStruct((b, s, NSA_WIDTH), BF16),
        scratch_shapes=[pltpu.VMEM((rows, LANES + n_sel), BF16),
                        pltpu.VMEM((rows, LANES), F32),
                        pltpu.VMEM((rows, LANES), F32)],
        compiler_params=_params("parallel", "parallel", "arbitrary"),
        name="nsa",
    )(q, kc, vc, k_aug, vs_aug, kw_p, vw_p, gate, mt)


def _conv_kernel(prev_ref, cur_ref, w_ref, b_ref, g_ref, beta_ref, o_ref, buf):
    tt = cur_ref.shape[1]
    i = pl.program_id(1)
    hist = prev_ref[0]
    buf[0:CONV_HALO] = jnp.where(i > 0, hist, jnp.zeros_like(hist))
    buf[CONV_HALO:] = cur_ref[0]
    off = CONV_HALO - (CONV_KSIZE - 1)
    acc = jnp.zeros((tt, CONV_WIDTH), F32) + b_ref[...]
    for k in range(CONV_KSIZE):
        acc = acc + buf[off + k:off + k + tt] * w_ref[k:k + 1]
    mu = jnp.mean(acc, axis=-1, keepdims=True)
    d = acc - mu
    var = jnp.mean(d * d, axis=-1, keepdims=True)
    y = d * lax.rsqrt(var + LN_EPS) * g_ref[...] + beta_ref[...]
    o_ref[0] = (y * _sigmoid(y)).astype(o_ref.dtype)


def _conv_module(u, conv_w, conv_b, ln_g, ln_b):
    b, s, _ = u.shape
    tt = min(TT_CONV, s)
    per = tt // CONV_HALO
    consts = [conv_w.reshape(CONV_KSIZE, CONV_WIDTH), conv_b[None, :], ln_g[None, :], ln_b[None, :]]
    return pl.pallas_call(
        _conv_kernel,
        grid=(b, s // tt),
        in_specs=[pl.BlockSpec((1, CONV_HALO, CONV_WIDTH), lambda bi, i: (bi, jnp.maximum(i * per - 1, 0), 0)),
                  pl.BlockSpec((1, tt, CONV_WIDTH), lambda bi, i: (bi, i, 0))]
                 + [_const_spec(c.shape) for c in consts],
        out_specs=pl.BlockSpec((1, tt, CONV_WIDTH), lambda bi, i: (bi, i, 0)),
        out_shape=jax.ShapeDtypeStruct((b, s, CONV_WIDTH), BF16),
        scratch_shapes=[pltpu.VMEM((CONV_HALO + tt, CONV_WIDTH), F32)],
        compiler_params=_params("parallel", "parallel"),
        name="conv",
    )(u, u, *consts)


def _memkv_kernel(mem_ref, g_ref, w_ref, bd_ref, gk_ref, km_ref, vm_ref):
    x = mem_ref[0]
    ms = jnp.mean(x * x, axis=-1, keepdims=True)
    xn = (x * lax.rsqrt(ms + RMS_EPS) * g_ref[...]).astype(BF16)
    kv = _dot(xn, w_ref[...])
    km_ref[0] = _head_norm(kv[:, :MEM_WIDTH], bd_ref[...], gk_ref[...]).astype(BF16)
    vm_ref[0] = kv[:, MEM_WIDTH:].astype(BF16)


def _memkv(mem, norm_mem, w_mem_kv, gain_k):
    b, n, _ = mem.shape
    consts = [norm_mem[None, :], w_mem_kv.astype(BF16), _block_diag(MEM_WIDTH, MEM_HEAD_DIM),
              jnp.tile(gain_k, MEM_HEADS)[None, :]]
    spec = pl.BlockSpec((1, n, MEM_WIDTH), lambda i: (i, 0, 0))
    return pl.pallas_call(
        _memkv_kernel,
        grid=(b,),
        in_specs=[pl.BlockSpec((1, n, D_MODEL), lambda i: (i, 0, 0))] + [_const_spec(c.shape) for c in consts],
        out_specs=[spec, spec],
        out_shape=[jax.ShapeDtypeStruct((b, n, MEM_WIDTH), BF16)] * 2,
        compiler_params=_params("parallel"),
        name="memkv",
    )(mem, *consts)


def _merge_kernel(x_ref, onsa_ref, uconv_ref, qm_ref, gm_ref, km_ref, vm_ref,
                  wn_ref, wc_ref, wmo_ref, wo_ref, h_ref):
    qm = qm_ref[...]
    km = km_ref[0]
    vm = vm_ref[0]
    heads = []
    for h in range(MEM_HEADS):
        c = slice(h * MEM_HEAD_DIM, (h + 1) * MEM_HEAD_DIM)
        s = _dot_nt(qm[:, c], km[:, c])
        e = jnp.exp(s - jnp.max(s, axis=-1, keepdims=True))
        p = e * (1.0 / jnp.sum(e, axis=-1, keepdims=True))
        heads.append(_dot(p.astype(BF16), vm[:, c]))
    o_m = jnp.concatenate(heads, axis=1).astype(BF16)
    merged = (gm_ref[:, 0:D_MODEL].astype(F32) * _dot(onsa_ref[...], wn_ref[...])
              + gm_ref[:, D_MODEL:2 * D_MODEL].astype(F32) * _dot(uconv_ref[...], wc_ref[...])
              + gm_ref[:, 2 * D_MODEL:3 * D_MODEL].astype(F32) * _dot(o_m, wmo_ref[...]))
    h_ref[...] = x_ref[...] + _dot(merged.astype(BF16), wo_ref[...])


def _merge(x2, o_nsa, u_conv, qm, gm, km, vm, w_nsa_out, w_conv_out, w_mem_out, w_out, seq):
    m = x2.shape[0]
    tm = min(TM_MERGE, seq)
    per_b = seq // tm
    n_mem = km.shape[1]
    consts = [w_nsa_out.astype(BF16), w_conv_out.astype(BF16), w_mem_out.astype(BF16), w_out.astype(BF16)]

    def row(n):
        return pl.BlockSpec((tm, n), lambda i: (i, 0))

    mem_spec = pl.BlockSpec((1, n_mem, MEM_WIDTH), lambda i: (i // per_b, 0, 0))
    return pl.pallas_call(
        _merge_kernel,
        grid=(m // tm,),
        in_specs=[row(D_MODEL), row(NSA_WIDTH), row(CONV_WIDTH), row(MEM_WIDTH), row(N_BRANCH * D_MODEL),
                  mem_spec, mem_spec] + [_const_spec(c.shape) for c in consts],
        out_specs=row(D_MODEL),
        out_shape=jax.ShapeDtypeStruct((m, D_MODEL), F32),
        compiler_params=_params("parallel"),
        name="merge",
    )(x2, o_nsa, u_conv, qm, gm, km, vm, *consts)


def _ffn_kernel(h_ref, g_ref, wg_ref, wu_ref, wd_ref, o_ref):
    h = h_ref[...]
    ms = jnp.mean(h * h, axis=-1, keepdims=True)
    hn = (h * lax.rsqrt(ms + RMS_EPS) * g_ref[...]).astype(BF16)
    acc = h
    for c in range(0, D_FF, FF_CHUNK):
        gate = _dot(hn, wg_ref[:, c:c + FF_CHUNK])
        up = _dot(hn, wu_ref[:, c:c + FF_CHUNK])
        act = (gate * _sigmoid(gate) * up).astype(BF16)
        acc = acc + _dot(act, wd_ref[c:c + FF_CHUNK, :])
    o_ref[...] = acc


def _ffn(h, norm_ffn, w_gate, w_up, w_down):
    m = h.shape[0]
    tm = min(TM_FFN, m)
    consts = [norm_ffn[None, :], w_gate.astype(BF16), w_up.astype(BF16), w_down.astype(BF16)]
    row = pl.BlockSpec((tm, D_MODEL), lambda i: (i, 0))
    return pl.pallas_call(
        _ffn_kernel,
        grid=(m // tm,),
        in_specs=[row] + [_const_spec(c.shape) for c in consts],
        out_specs=row,
        out_shape=jax.ShapeDtypeStruct((m, D_MODEL), F32),
        compiler_params=_params("parallel"),
        name="ffn",
    )(h, *consts)


def _layer(x, mem, norm_mix, w_in, nsa_qk_norm, cmp_pos, cmp_w1, cmp_w2, w_nsa_out,
           conv_w, conv_b, conv_ln_g, conv_ln_b, w_conv_out, norm_mem, w_mem_kv,
           mem_qk_norm, w_mem_out, w_out, norm_ffn, w_gate, w_up, w_down):
    b, s, _ = x.shape
    x2 = x.reshape(b * s, D_MODEL)
    q, kc_raw, vc_raw, ks, vs, kw, vw, gn, u, qm, gm = _inproj(x2, norm_mix, w_in, nsa_qk_norm, mem_qk_norm)

    def group_major(t):
        return jnp.transpose(t.reshape(b, s, NSA_GROUPS, HEAD_DIM), (0, 2, 1, 3))

    def chunked(t):
        return group_major(t).reshape(b, NSA_GROUPS, s // CMP_STRIDE, CMP_STRIDE * HEAD_DIM)

    kc = _compress(chunked(kc_raw), cmp_pos[0], cmp_w1[0], cmp_w2[0], nsa_qk_norm[1], True)
    vc = _compress(chunked(vc_raw), cmp_pos[1], cmp_w1[1], cmp_w2[1], nsa_qk_norm[1], False)
    q_h = jnp.transpose(q.reshape(b, s, NSA_HEADS, HEAD_DIM), (0, 2, 1, 3))
    gate = gn[:, :3 * NSA_HEADS].reshape(b, s, NSA_GROUPS, 3 * NSA_HPG)
    gate = jnp.pad(jnp.transpose(gate, (0, 2, 1, 3)), ((0, 0), (0, 0), (0, 0), (0, GATE_PAD - 3 * NSA_HPG)))
    o_nsa = _nsa(q_h, kc, vc, group_major(ks), group_major(vs), group_major(kw), group_major(vw), gate)

    u_conv = _conv_module(u.reshape(b, s, CONV_WIDTH), conv_w, conv_b, conv_ln_g, conv_ln_b)
    km, vm = _memkv(mem, norm_mem, w_mem_kv, mem_qk_norm[1])
    h = _merge(x2, o_nsa.reshape(b * s, NSA_WIDTH), u_conv.reshape(b * s, CONV_WIDTH), qm, gm, km, vm,
               w_nsa_out, w_conv_out, w_mem_out, w_out, s)
    out = _ffn(h, norm_ffn, w_gate, w_up, w_down)
    return out.reshape(b, s, D_MODEL)


def kernel(x, mem, norm_mix, w_in, nsa_qk_norm, cmp_pos, cmp_w1, cmp_w2, w_nsa_out, conv_w, conv_b, conv_ln_g, conv_ln_b, w_conv_out, norm_mem, w_mem_kv, mem_qk_norm, w_mem_out, w_out, norm_ffn, w_gate, w_up, w_down):
    h = x
    for l in range(norm_mix.shape[0]):
        h = _layer(h, mem, norm_mix[l], w_in[l], nsa_qk_norm[l], cmp_pos[l], cmp_w1[l], cmp_w2[l],
                   w_nsa_out[l], conv_w[l], conv_b[l], conv_ln_g[l], conv_ln_b[l], w_conv_out[l],
                   norm_mem[l], w_mem_kv[l], mem_qk_norm[l], w_mem_out[l], w_out[l], norm_ffn[l],
                   w_gate[l], w_up[l], w_down[l])
    return h
```

```python
import functools

import numpy as np
import jax
import jax.numpy as jnp
from jax import lax
from jax.experimental import pallas as pl
from jax.experimental.pallas import tpu as pltpu

F32 = jnp.float32
BF16 = jnp.bfloat16

D_MODEL = 1024
HEAD_DIM = 64
NSA_HEADS = 8
NSA_GROUPS = 2
NSA_HPG = NSA_HEADS // NSA_GROUPS
NSA_WIDTH = NSA_HEADS * HEAD_DIM
KV_WIDTH = NSA_GROUPS * HEAD_DIM
CMP_BLOCK = 32
CMP_STRIDE = 16
CMP_HIDDEN = 256
SLC_BLOCK = 64
SLC_TOPK = 16
WINDOW = 512
FORCE = 1e6
CONV_WIDTH = 512
CONV_KSIZE = 31
MEM_HEADS = 4
MEM_HEAD_DIM = 128
MEM_WIDTH = MEM_HEADS * MEM_HEAD_DIM
N_BRANCH = 3
D_FF = 2816
RMS_EPS = 1e-6
LN_EPS = 1e-5

LANES = 128
F32_SUBLANES = 8
VMEM_LIMIT = 56 * 1024 * 1024

TM_PROJ = 256
TQ = 256
TK = 1024
WSUB = 128
NSA_CHAINS = 1
LOG2E = 1.4426950408889634
TT_CONV = 512
CONV_HALO = 32
CONV_ROWS = 64
TM_MERGE = 512
TM_FFN = 512
FF_CHUNK = D_FF // 2
MASKED = -1e30
SEL_BIAS = -1e9
SEL_PAD = TK - TQ
W_ONE = HEAD_DIM + 1


def _dot(a, b):
    return jnp.dot(a, b, preferred_element_type=F32)


def _dot_nt(a, b):
    return lax.dot_general(a, b, (((1,), (1,)), ((), ())), preferred_element_type=F32)


def _head_norm(t, bd, gain):
    ms = _dot((t * t).astype(BF16), bd)
    return t * lax.rsqrt(ms + RMS_EPS) * gain


def _sigmoid(x):
    return 1.0 / (1.0 + jnp.exp(-x))


def _const_spec(shape):
    nd = len(shape)
    return pl.BlockSpec(shape, lambda *_: (0,) * nd, pipeline_mode=pl.Buffered(1))


def _params(*sem):
    return pltpu.CompilerParams(dimension_semantics=sem, vmem_limit_bytes=VMEM_LIMIT)


def _inproj_kernel(x_ref, g_ref, wq_ref, wkv_ref, wg_ref, wc_ref, wm_ref, wgm_ref,
                   bdq_ref, bdk_ref, bdm_ref, gq_ref, gks_ref, gkw_ref, gqm_ref,
                   q_ref, kc_ref, vc_ref, ks_ref, vs_ref, kw_ref, vw_ref, gn_ref, u_ref, qm_ref, gm_ref,
                   *, tiles_per_seq):
    tm = x_ref.shape[0]
    n_sel = ks_ref.shape[3] - LANES
    x = x_ref[...]
    ms = jnp.mean(x * x, axis=-1, keepdims=True)
    xn = (x * lax.rsqrt(ms + RMS_EPS) * g_ref[...]).astype(BF16)

    qn = _head_norm(_dot(xn, wq_ref[...]), bdq_ref[...], gq_ref[...]).astype(BF16)
    for h in range(NSA_HEADS):
        q_ref[0, h] = qn[:, h * HEAD_DIM:(h + 1) * HEAD_DIM]

    pkv = _dot(xn, wkv_ref[...])
    k_c = pkv[:, 0 * KV_WIDTH:1 * KV_WIDTH]
    v_c = pkv[:, 1 * KV_WIDTH:2 * KV_WIDTH]
    k_s = _head_norm(pkv[:, 2 * KV_WIDTH:3 * KV_WIDTH], bdk_ref[...], gks_ref[...]).astype(BF16)
    v_s = pkv[:, 3 * KV_WIDTH:4 * KV_WIDTH].astype(BF16)
    k_w = _head_norm(pkv[:, 4 * KV_WIDTH:5 * KV_WIDTH], bdk_ref[...], gkw_ref[...]).astype(BF16)
    v_w = pkv[:, 5 * KV_WIDTH:6 * KV_WIDTH].astype(BF16)
    gates = _sigmoid(_dot(xn, wg_ref[...])).astype(BF16)

    t0 = (pl.program_id(0) % tiles_per_seq) * tm
    blk_of_row = (t0 + lax.broadcasted_iota(jnp.int32, (tm, n_sel), 0)) // SLC_BLOCK
    onehot = jnp.where(blk_of_row == lax.broadcasted_iota(jnp.int32, (tm, n_sel), 1), 1.0, 0.0).astype(BF16)
    rest = LANES - HEAD_DIM
    zeros = jnp.zeros((tm, rest), BF16)
    rest_lane = lax.broadcasted_iota(jnp.int32, (tm, rest), 1)
    ones_sel = jnp.where(rest_lane == 0, 1.0, 0.0).astype(BF16)
    ones_win = jnp.where(rest_lane == W_ONE - HEAD_DIM, 1.0, 0.0).astype(BF16)
    for g in range(NSA_GROUPS):
        c = slice(g * HEAD_DIM, (g + 1) * HEAD_DIM)
        kc_ref[0, g] = k_c[:, c]
        vc_ref[0, g] = v_c[:, c]
        ks_ref[0, g, :, 0:HEAD_DIM] = k_s[:, c]
        ks_ref[0, g, :, HEAD_DIM:LANES] = zeros
        ks_ref[0, g, :, LANES:] = onehot
        vs_ref[0, g, :, 0:HEAD_DIM] = v_s[:, c]
        vs_ref[0, g, :, HEAD_DIM:] = ones_sel
        kw_ref[0, g, :, 0:HEAD_DIM] = k_w[:, c]
        kw_ref[0, g, :, HEAD_DIM:] = zeros
        vw_ref[0, g, :, 0:HEAD_DIM] = v_w[:, c]
        vw_ref[0, g, :, HEAD_DIM:] = ones_win
        gn_ref[0, g] = gates[:, g * LANES:(g + 1) * LANES]

    pc = _dot(xn, wc_ref[...])
    u_ref[...] = pc[:, :CONV_WIDTH] * _sigmoid(pc[:, CONV_WIDTH:])

    qm_ref[...] = _head_norm(_dot(xn, wm_ref[...]), bdm_ref[...], gqm_ref[...]).astype(BF16)

    gm_ref[...] = _sigmoid(_dot(xn, wgm_ref[...])).astype(BF16)


def _block_diag(n, blk):
    idx = np.arange(n) // blk
    return jnp.asarray((idx[:, None] == idx[None, :]).astype(np.float32) / blk, dtype=BF16)


def _inproj(x, norm_mix, w_in, nsa_qk_norm, mem_qk_norm):
    b, s, _ = x.shape
    m = b * s
    tm = min(TM_PROJ, s)
    per_seq = s // tm
    n_sel = s // SLC_BLOCK
    sizes = [NSA_WIDTH, 6 * KV_WIDTH, 3 * NSA_HEADS, 2 * CONV_WIDTH, MEM_WIDTH, N_BRANCH * D_MODEL]
    pts = np.cumsum(sizes)[:-1]
    wq, wkv, wg, wc, wm, wgm = [w.astype(BF16) for w in jnp.split(w_in, pts, axis=1)]
    per_group = 3 * NSA_HPG
    wg = jnp.pad(wg.reshape(D_MODEL, NSA_GROUPS, per_group), ((0, 0), (0, 0), (0, LANES - per_group)))
    wg = wg.reshape(D_MODEL, NSA_GROUPS * LANES)
    gq = (jnp.tile(nsa_qk_norm[0], NSA_HEADS) * (HEAD_DIM ** -0.5 * LOG2E))[None, :]
    gks = jnp.tile(nsa_qk_norm[2], NSA_GROUPS)[None, :]
    gkw = jnp.tile(nsa_qk_norm[3], NSA_GROUPS)[None, :]
    gqm = (jnp.tile(mem_qk_norm[0], MEM_HEADS) * (MEM_HEAD_DIM ** -0.5))[None, :]
    bdq = _block_diag(NSA_WIDTH, HEAD_DIM)
    bdk = _block_diag(KV_WIDTH, HEAD_DIM)
    bdm = _block_diag(MEM_WIDTH, MEM_HEAD_DIM)
    consts = [norm_mix[None, :], wq, wkv, wg, wc, wm, wgm, bdq, bdk, bdm, gq, gks, gkw, gqm]

    def row(n):
        return pl.BlockSpec((tm, n), lambda i: (i, 0))

    def heads(n_heads, width):
        return pl.BlockSpec((1, n_heads, tm, width), lambda i: (i // per_seq, 0, i % per_seq, 0))

    grp = NSA_GROUPS
    outs = [((b, NSA_HEADS, s, HEAD_DIM), BF16, heads(NSA_HEADS, HEAD_DIM)),
            ((b, grp, s, HEAD_DIM), F32, heads(grp, HEAD_DIM)),
            ((b, grp, s, HEAD_DIM), F32, heads(grp, HEAD_DIM)),
            ((b, grp, s, LANES + n_sel), BF16, heads(grp, LANES + n_sel)),
            ((b, grp, s, LANES), BF16, heads(grp, LANES)),
            ((b, grp, s, LANES), BF16, heads(grp, LANES)),
            ((b, grp, s, LANES), BF16, heads(grp, LANES)),
            ((b, grp, s, LANES), BF16, heads(grp, LANES)),
            ((m, CONV_WIDTH), F32, row(CONV_WIDTH)),
            ((m, MEM_WIDTH), BF16, row(MEM_WIDTH)),
            ((m, N_BRANCH * D_MODEL), BF16, row(N_BRANCH * D_MODEL))]
    return pl.pallas_call(
        functools.partial(_inproj_kernel, tiles_per_seq=per_seq),
        grid=(m // tm,),
        in_specs=[row(D_MODEL)] + [_const_spec(c.shape) for c in consts],
        out_specs=[spec for _, _, spec in outs],
        out_shape=[jax.ShapeDtypeStruct(shape, dt) for shape, dt, _ in outs],
        compiler_params=_params("parallel"),
        name="inproj",
    )(x.reshape(m, D_MODEL), *consts)


def _gelu_tanh(x):
    return 0.5 * x * (1.0 + jnp.tanh(np.sqrt(2.0 / np.pi) * (x + 0.044715 * (x * x * x))))


def _compress_kernel(x_ref, plo_ref, phi_ref, w1a_ref, w1b_ref, w2_ref, gain_ref, o_ref, *, normalize):
    x = x_ref[0, 0]
    a = _dot((x + plo_ref[...]).astype(BF16), w1a_ref[...])
    b = _dot((x + phi_ref[...]).astype(BF16), w1b_ref[...])
    n = x.shape[0]
    h = a + pltpu.roll(b, n - 1, 0)
    y = _dot(_gelu_tanh(h).astype(BF16), w2_ref[...])
    if normalize:
        ms = jnp.mean(y * y, axis=-1, keepdims=True)
        y = y * lax.rsqrt(ms + RMS_EPS) * gain_ref[...]
    o_ref[0, 0] = y.astype(o_ref.dtype)


def _compress(xc, pos, w1, w2, gain, normalize):
    b, g, n, width = xc.shape
    half = CMP_STRIDE * HEAD_DIM
    plo = pos[:CMP_STRIDE].reshape(1, half)
    phi = pos[CMP_STRIDE:].reshape(1, half)
    w1a = w1[:half].astype(BF16)
    w1b = w1[half:].astype(BF16)
    consts = [plo, phi, w1a, w1b, w2.astype(BF16), gain[None, :]]
    return pl.pallas_call(
        functools.partial(_compress_kernel, normalize=normalize),
        grid=(b, g),
        in_specs=[pl.BlockSpec((1, 1, n, width), lambda i, j: (i, j, 0, 0))] + [_const_spec(c.shape) for c in consts],
        out_specs=pl.BlockSpec((1, 1, n, HEAD_DIM), lambda i, j: (i, j, 0, 0)),
        out_shape=jax.ShapeDtypeStruct((b, g, n, HEAD_DIM), BF16),
        compiler_params=_params("parallel", "parallel"),
        name="compress_k" if normalize else "compress_v",
    )(xc, *consts)


def _nsa_kernel(q_ref, kc_ref, vc_ref, ks_ref, vs_ref, kw_ref, vw_ref, gate_ref, mt_ref, ex_ref, o_ref,
                qaug_scr, m_scr, acc_scr, *, n_cmp, n_sel):
    tq = q_ref.shape[2]
    hpc = NSA_HPG // NSA_CHAINS
    crow = hpc * tq
    chains = range(NSA_CHAINS)
    i = pl.program_id(2)
    q0 = i * tq

    def q_chain(r):
        return q_ref[0, r * hpc:(r + 1) * hpc].reshape(crow, HEAD_DIM)

    def rows_of(r):
        return slice(r * crow, (r + 1) * crow)

    def stack(a, n=hpc):
        return jnp.concatenate([a] * n, axis=0)

    def causal_bias(n, keep):
        r_i = lax.broadcasted_iota(jnp.int32, (n, n), 0)
        c_i = lax.broadcasted_iota(jnp.int32, (n, n), 1)
        return jnp.where(keep(r_i, c_i), 0.0, MASKED)

    t_col = q0 + lax.broadcasted_iota(jnp.int32, (tq, 1), 0)

    rest = LANES - HEAD_DIM
    flag_col = jnp.where(lax.broadcasted_iota(jnp.int32, (crow, rest), 1) == 0, SEL_BIAS, 0.0).astype(BF16)
    for r in chains:
        qaug_scr[rows_of(r), 0:HEAD_DIM] = q_chain(r)
        qaug_scr[rows_of(r), HEAD_DIM:LANES] = flag_col

    cmp_end = lax.broadcasted_iota(jnp.int32, (1, n_cmp), 1) * CMP_STRIDE + (CMP_BLOCK - 1)
    cbias = stack(jnp.where(cmp_end <= t_col, 0.0, MASKED))
    seen = stack(jnp.where(t_col >= CMP_BLOCK - 1, 1.0, 0.0))
    o_c = []
    psum = None
    for r in chains:
        s = _dot_nt(q_chain(r), kc_ref[0, 0]) + cbias
        e = jnp.exp2(s - jnp.max(s, axis=-1, keepdims=True))
        p = e * (seen / jnp.sum(e, axis=-1, keepdims=True))
        o_c.append(_dot(p.astype(BF16), vc_ref[0, 0]))
        for h in range(hpc):
            ph = p[h * tq:(h + 1) * tq]
            psum = ph if psum is None else psum + ph

    hi = psum.astype(BF16)
    lo = (psum - hi.astype(F32)).astype(BF16)
    imp = _dot_nt(mt_ref[...], hi) + _dot_nt(mt_ref[...], lo)
    blk = lax.broadcasted_iota(jnp.int32, (n_sel, tq), 0)
    t_row = q0 + lax.broadcasted_iota(jnp.int32, (n_sel, tq), 1)
    cur = t_row // SLC_BLOCK
    forced = (blk == 0) | (blk == cur) | (blk == cur - 1)
    future = blk * SLC_BLOCK > t_row
    n_forced = 3
    v = jnp.where(forced, -jnp.inf, jnp.where(future, -FORCE, imp))
    blk_f = blk.astype(F32)

    def extract(v, n):
        for _ in range(n):
            top = jnp.max(v, axis=0, keepdims=True)
            first = jnp.min(jnp.where(v == top, blk_f, float(n_sel)), axis=0, keepdims=True)
            v = jnp.where(blk_f == first, -jnp.inf, v)
        return v

    span = WINDOW + WSUB
    pair = 2
    enter = stack(causal_bias(WSUB, lambda r_i, c_i: c_i > r_i), pair)
    leave = stack(causal_bias(WSUB, lambda r_i, c_i: c_i <= r_i), pair)
    pieces = [(sub, hp) for sub in range(tq // WSUB) for hp in range(NSA_HPG // pair)]
    todo = min(SLC_TOPK, n_sel) - n_forced
    per_piece = -(-todo // len(pieces))
    o_w = {}
    for sub, hp in pieces:
        qw = jnp.concatenate([qaug_scr[h * tq + sub * WSUB:h * tq + (sub + 1) * WSUB, 0:LANES]
                              for h in range(hp * pair, (hp + 1) * pair)], axis=0)
        start = pl.multiple_of(q0 + sub * WSUB, WSUB)
        sw = _dot_nt(qw, kw_ref[0, 0, pl.ds(start, span), :])
        sw = jnp.concatenate([sw[:, :WSUB] + enter, sw[:, WSUB:WINDOW], sw[:, WINDOW:] + leave], axis=1)
        pw = jnp.exp2(sw - jnp.max(sw, axis=-1, keepdims=True))
        o_w[sub, hp] = _dot(pw.astype(BF16), vw_ref[0, 0, pl.ds(start, span), :])
        v = extract(v, min(per_piece, todo))
        todo -= min(per_piece, todo)
    bias_t = jnp.where(future, SEL_BIAS, jnp.where(v == -jnp.inf, 0.0, SEL_BIAS))
    bias = stack(jnp.transpose(bias_t).astype(BF16))

    for r in chains:
        qaug_scr[rows_of(r), LANES:LANES + n_sel] = bias
    m_scr[...] = jnp.full(m_scr.shape, MASKED, F32)
    acc_scr[...] = jnp.zeros(acc_scr.shape, F32)

    def sel_chunk(k0, tail_bias):
        for r in chains:
            sc = _dot_nt(qaug_scr[rows_of(r)], ks_ref[0, 0, pl.ds(k0, TK), :])
            if tail_bias is not None:
                sc = jnp.concatenate([sc[:, :TK - tq], sc[:, TK - tq:] + tail_bias], axis=1)
            m_old = m_scr[rows_of(r)]
            m_new = jnp.maximum(m_old, jnp.max(sc, axis=-1, keepdims=True))
            pc = jnp.exp2(sc - jnp.concatenate([m_new] * (TK // LANES), axis=1))
            acc_scr[rows_of(r)] = (jnp.exp2(m_old - m_new) * acc_scr[rows_of(r)]
                                   + _dot(pc.astype(BF16), vs_ref[0, 0, pl.ds(k0, TK), :]))
            m_scr[rows_of(r)] = m_new

    n_low = q0 // TK
    low0 = q0 - n_low * TK

    def body(c, carry):
        sel_chunk(pl.multiple_of(low0 + c * TK, tq), None)
        return carry

    lax.fori_loop(0, n_low, body, 0)
    sel_chunk(pl.multiple_of(q0, tq), stack(causal_bias(tq, lambda r_i, c_i: c_i <= r_i)))

    lane = lax.broadcasted_iota(jnp.int32, (tq, LANES), 1)
    gates = gate_ref[0, 0].astype(F32)
    for r in chains:
        acc = acc_scr[rows_of(r)]
        for hh in range(hpc):
            h = r * hpc + hh
            rs = slice(hh * tq, (hh + 1) * tq)
            a_s = acc[rs]
            a_w = jnp.concatenate([o_w[sub, h // pair][(h % pair) * WSUB:(h % pair + 1) * WSUB]
                                   for sub in range(tq // WSUB)], axis=0)
            scal = jnp.where(lane == HEAD_DIM, 1.0 / a_s, jnp.where(lane == W_ONE, 1.0 / a_w, gates))
            y = _dot(scal.astype(BF16), ex_ref[h])

            def slot(k):
                return y[:, k * LANES:k * LANES + HEAD_DIM]

            out = (slot(0) * o_c[r][rs] + (slot(1) * slot(3)) * a_s[:, :HEAD_DIM]
                   + (slot(2) * slot(4)) * a_w[:, :HEAD_DIM])
            o_ref[0, :, h * HEAD_DIM:(h + 1) * HEAD_DIM] = out.astype(o_ref.dtype)


def _importance_matrix(n_cmp, n_sel):
    ratio = SLC_BLOCK // CMP_STRIDE
    c = np.arange(n_cmp)[None, :]
    j = np.arange(n_sel)[:, None]
    return jnp.asarray(((c >= ratio * j - 1) & (c <= ratio * j + ratio - 1)).astype(np.float32), dtype=BF16)


def _spread_matrix():
    e = np.zeros((NSA_HPG, LANES, 5 * LANES), np.float32)
    for h in range(NSA_HPG):
        for k, src in enumerate([3 * h, 3 * h + 1, 3 * h + 2, HEAD_DIM, W_ONE]):
            e[h, src, k * LANES:k * LANES + HEAD_DIM] = 1.0
    return jnp.asarray(e, dtype=BF16)


def _nsa(q, kc, vc, k_aug, vs_aug, kw_aug, vw_aug, gate):
    b, _, s, _ = q.shape
    tq = min(TQ, s)
    n_cmp = s // CMP_STRIDE
    n_sel = s // SLC_BLOCK
    assert n_sel <= LANES and s % TK == 0 and TK % tq == 0 and tq % WSUB == 0 and SEL_PAD == TK - tq

    def left_pad(t, rows, flagged):
        pad = jnp.zeros((rows, t.shape[-1]), BF16)
        if flagged:
            pad = pad.at[:, HEAD_DIM].set(1.0)
        return jnp.concatenate([jnp.broadcast_to(pad, t.shape[:2] + pad.shape), t], axis=2)

    ks_p = left_pad(k_aug, SEL_PAD, True)
    vs_p = left_pad(vs_aug, SEL_PAD, False)
    kw_p = left_pad(kw_aug, WINDOW, True)
    vw_p = left_pad(vw_aug, WINDOW, False)
    mt = _importance_matrix(n_cmp, n_sel)
    ex = _spread_matrix()
    rows = NSA_HPG * tq

    def per_group(shape):
        return pl.BlockSpec((1, 1) + shape, lambda bi, gi, i: (bi, gi, 0, 0))

    return pl.pallas_call(
        functools.partial(_nsa_kernel, n_cmp=n_cmp, n_sel=n_sel),
        grid=(b, NSA_GROUPS, s // tq),
        in_specs=[
            pl.BlockSpec((1, NSA_HPG, tq, HEAD_DIM), lambda bi, gi, i: (bi, gi, i, 0)),
            per_group((n_cmp, HEAD_DIM)), per_group((n_cmp, HEAD_DIM)),
            per_group((s + SEL_PAD, LANES + n_sel)), per_group((s + SEL_PAD, LANES)),
            per_group((s + WINDOW, LANES)), per_group((s + WINDOW, LANES)),
            pl.BlockSpec((1, 1, tq, LANES), lambda bi, gi, i: (bi, gi, i, 0)),
            _const_spec(mt.shape), _const_spec(ex.shape),
        ],
        out_specs=pl.BlockSpec((1, tq, NSA_HPG * HEAD_DIM), lambda bi, gi, i: (bi, i, gi)),
        out_shape=jax.ShapeDtypeStruct((b, s, NSA_WIDTH), BF16),
        scratch_shapes=[pltpu.VMEM((rows, LANES + n_sel), BF16),
                        pltpu.VMEM((rows, LANES), F32),
                        pltpu.VMEM((rows, LANES), F32)],
        compiler_params=_params("parallel", "parallel", "arbitrary"),
        name="nsa",
    )(q, kc, vc, ks_p, vs_p, kw_p, vw_p, gate, mt, ex)


def _conv_kernel(prev_ref, cur_ref, w_ref, b_ref, g_ref, beta_ref, o_ref, buf, shifted):
    tt = cur_ref.shape[1]
    i = pl.program_id(1)
    hist = prev_ref[0]
    buf[0:CONV_HALO] = jnp.where(i > 0, hist, jnp.zeros_like(hist))
    buf[CONV_HALO:] = cur_ref[0]
    off = CONV_HALO - (CONV_KSIZE - 1)
    for rho in range(1, F32_SUBLANES):
        shifted[rho - 1] = buf[rho:rho + shifted.shape[1]]
    for r0 in range(0, tt, CONV_ROWS):
        acc = jnp.zeros((CONV_ROWS, CONV_WIDTH), F32) + b_ref[...]
        for k in range(CONV_KSIZE):
            rho = (off + k) % F32_SUBLANES
            a = r0 + off + k - rho
            src = buf[a:a + CONV_ROWS] if rho == 0 else shifted[rho - 1, a:a + CONV_ROWS]
            acc = acc + src * w_ref[k:k + 1]
        mu = jnp.mean(acc, axis=-1, keepdims=True)
        d = acc - mu
        var = jnp.mean(d * d, axis=-1, keepdims=True)
        y = d * lax.rsqrt(var + LN_EPS) * g_ref[...] + beta_ref[...]
        o_ref[0, r0:r0 + CONV_ROWS] = (y * _sigmoid(y)).astype(o_ref.dtype)


def _conv_module(u, conv_w, conv_b, ln_g, ln_b):
    b, s, _ = u.shape
    tt = min(TT_CONV, s)
    per = tt // CONV_HALO
    consts = [conv_w.reshape(CONV_KSIZE, CONV_WIDTH), conv_b[None, :], ln_g[None, :], ln_b[None, :]]
    return pl.pallas_call(
        _conv_kernel,
        grid=(b, s // tt),
        in_specs=[pl.BlockSpec((1, CONV_HALO, CONV_WIDTH), lambda bi, i: (bi, jnp.maximum(i * per - 1, 0), 0)),
                  pl.BlockSpec((1, tt, CONV_WIDTH), lambda bi, i: (bi, i, 0))]
                 + [_const_spec(c.shape) for c in consts],
        out_specs=pl.BlockSpec((1, tt, CONV_WIDTH), lambda bi, i: (bi, i, 0)),
        out_shape=jax.ShapeDtypeStruct((b, s, CONV_WIDTH), BF16),
        scratch_shapes=[pltpu.VMEM((CONV_HALO + tt, CONV_WIDTH), F32),
                        pltpu.VMEM((F32_SUBLANES - 1, tt + CONV_HALO - F32_SUBLANES, CONV_WIDTH), F32)],
        compiler_params=_params("parallel", "parallel"),
        name="conv",
    )(u, u, *consts)


def _memkv_kernel(mem_ref, g_ref, w_ref, bd_ref, gk_ref, km_ref, vm_ref):
    x = mem_ref[0]
    ms = jnp.mean(x * x, axis=-1, keepdims=True)
    xn = (x * lax.rsqrt(ms + RMS_EPS) * g_ref[...]).astype(BF16)
    kv = _dot(xn, w_ref[...])
    km_ref[0] = _head_norm(kv[:, :MEM_WIDTH], bd_ref[...], gk_ref[...]).astype(BF16)
    vm_ref[0] = kv[:, MEM_WIDTH:].astype(BF16)


def _memkv(mem, norm_mem, w_mem_kv, gain_k):
    b, n, _ = mem.shape
    consts = [norm_mem[None, :], w_mem_kv.astype(BF16), _block_diag(MEM_WIDTH, MEM_HEAD_DIM),
              jnp.tile(gain_k, MEM_HEADS)[None, :]]
    spec = pl.BlockSpec((1, n, MEM_WIDTH), lambda i: (i, 0, 0))
    return pl.pallas_call(
        _memkv_kernel,
        grid=(b,),
        in_specs=[pl.BlockSpec((1, n, D_MODEL), lambda i: (i, 0, 0))] + [_const_spec(c.shape) for c in consts],
        out_specs=[spec, spec],
        out_shape=[jax.ShapeDtypeStruct((b, n, MEM_WIDTH), BF16)] * 2,
        compiler_params=_params("parallel"),
        name="memkv",
    )(mem, *consts)


def _merge_kernel(x_ref, onsa_ref, uconv_ref, qm_ref, gm_ref, km_ref, vm_ref,
                  wn_ref, wc_ref, wmo_ref, wo_ref, h_ref):
    qm = qm_ref[...]
    km = km_ref[0]
    vm = vm_ref[0]
    heads = []
    for h in range(MEM_HEADS):
        c = slice(h * MEM_HEAD_DIM, (h + 1) * MEM_HEAD_DIM)
        s = _dot_nt(qm[:, c], km[:, c])
        e = jnp.exp(s - jnp.max(s, axis=-1, keepdims=True))
        p = e * (1.0 / jnp.sum(e, axis=-1, keepdims=True))
        heads.append(_dot(p.astype(BF16), vm[:, c]))
    o_m = jnp.concatenate(heads, axis=1).astype(BF16)
    merged = (gm_ref[:, 0:D_MODEL].astype(F32) * _dot(onsa_ref[...], wn_ref[...])
              + gm_ref[:, D_MODEL:2 * D_MODEL].astype(F32) * _dot(uconv_ref[...], wc_ref[...])
              + gm_ref[:, 2 * D_MODEL:3 * D_MODEL].astype(F32) * _dot(o_m, wmo_ref[...]))
    h_ref[...] = x_ref[...] + _dot(merged.astype(BF16), wo_ref[...])


def _merge(x2, o_nsa, u_conv, qm, gm, km, vm, w_nsa_out, w_conv_out, w_mem_out, w_out, seq):
    m = x2.shape[0]
    tm = min(TM_MERGE, seq)
    per_b = seq // tm
    n_mem = km.shape[1]
    consts = [w_nsa_out.astype(BF16), w_conv_out.astype(BF16), w_mem_out.astype(BF16), w_out.astype(BF16)]

    def row(n):
        return pl.BlockSpec((tm, n), lambda i: (i, 0))

    mem_spec = pl.BlockSpec((1, n_mem, MEM_WIDTH), lambda i: (i // per_b, 0, 0))
    return pl.pallas_call(
        _merge_kernel,
        grid=(m // tm,),
        in_specs=[row(D_MODEL), row(NSA_WIDTH), row(CONV_WIDTH), row(MEM_WIDTH), row(N_BRANCH * D_MODEL),
                  mem_spec, mem_spec] + [_const_spec(c.shape) for c in consts],
        out_specs=row(D_MODEL),
        out_shape=jax.ShapeDtypeStruct((m, D_MODEL), F32),
        compiler_params=_params("parallel"),
        name="merge",
    )(x2, o_nsa, u_conv, qm, gm, km, vm, *consts)


def _ffn_kernel(h_ref, g_ref, wg_ref, wu_ref, wd_ref, o_ref):
    h = h_ref[...]
    ms = jnp.mean(h * h, axis=-1, keepdims=True)
    hn = (h * lax.rsqrt(ms + RMS_EPS) * g_ref[...]).astype(BF16)
    acc = h
    for c in range(0, D_FF, FF_CHUNK):
        gate = _dot(hn, wg_ref[:, c:c + FF_CHUNK])
        up = _dot(hn, wu_ref[:, c:c + FF_CHUNK])
        act = (gate * _sigmoid(gate) * up).astype(BF16)
        acc = acc + _dot(act, wd_ref[c:c + FF_CHUNK, :])
    o_ref[...] = acc


def _ffn(h, norm_ffn, w_gate, w_up, w_down):
    m = h.shape[0]
    tm = min(TM_FFN, m)
    consts = [norm_ffn[None, :], w_gate.astype(BF16), w_up.astype(BF16), w_down.astype(BF16)]
    row = pl.BlockSpec((tm, D_MODEL), lambda i: (i, 0))
    return pl.pallas_call(
        _ffn_kernel,
        grid=(m // tm,),
        in_specs=[row] + [_const_spec(c.shape) for c in consts],
        out_specs=row,
        out_shape=jax.ShapeDtypeStruct((m, D_MODEL), F32),
        compiler_params=_params("parallel"),
        name="ffn",
    )(h, *consts)


def _layer(x, mem, norm_mix, w_in, nsa_qk_norm, cmp_pos, cmp_w1, cmp_w2, w_nsa_out,
           conv_w, conv_b, conv_ln_g, conv_ln_b, w_conv_out, norm_mem, w_mem_kv,
           mem_qk_norm, w_mem_out, w_out, norm_ffn, w_gate, w_up, w_down):
    b, s, _ = x.shape
    x2 = x.reshape(b * s, D_MODEL)
    q, kc_raw, vc_raw, k_aug, vs_aug, kw_aug, vw_aug, gate, u, qm, gm = _inproj(
        x, norm_mix, w_in, nsa_qk_norm, mem_qk_norm)

    def chunked(t):
        return t.reshape(b, NSA_GROUPS, s // CMP_STRIDE, CMP_STRIDE * HEAD_DIM)

    kc = _compress(chunked(kc_raw), cmp_pos[0], cmp_w1[0], cmp_w2[0], nsa_qk_norm[1], True)
    vc = _compress(chunked(vc_raw), cmp_pos[1], cmp_w1[1], cmp_w2[1], nsa_qk_norm[1], False)
    o_nsa = _nsa(q, kc, vc, k_aug, vs_aug, kw_aug, vw_aug, gate)

    u_conv = _conv_module(u.reshape(b, s, CONV_WIDTH), conv_w, conv_b, conv_ln_g, conv_ln_b)
    km, vm = _memkv(mem, norm_mem, w_mem_kv, mem_qk_norm[1])
    h = _merge(x2, o_nsa.reshape(b * s, NSA_WIDTH), u_conv.reshape(b * s, CONV_WIDTH), qm, gm, km, vm,
               w_nsa_out, w_conv_out, w_mem_out, w_out, s)
    out = _ffn(h, norm_ffn, w_gate, w_up, w_down)
    return out.reshape(b, s, D_MODEL)


def kernel(x, mem, norm_mix, w_in, nsa_qk_norm, cmp_pos, cmp_w1, cmp_w2, w_nsa_out, conv_w, conv_b, conv_ln_g, conv_ln_b, w_conv_out, norm_mem, w_mem_kv, mem_qk_norm, w_mem_out, w_out, norm_ffn, w_gate, w_up, w_down):
    h = x
    for l in range(norm_mix.shape[0]):
        h = _layer(h, mem, norm_mix[l], w_in[l], nsa_qk_norm[l], cmp_pos[l], cmp_w1[l], cmp_w2[l],
                   w_nsa_out[l], conv_w[l], conv_b[l], conv_ln_g[l], conv_ln_b[l], w_conv_out[l],
                   norm_mem[l], w_mem_kv[l], mem_qk_norm[l], w_mem_out[l], w_out[l], norm_ffn[l],
                   w_gate[l], w_up[l], w_down[l])
    return h
```

```python
import functools

import numpy as np
import jax
import jax.numpy as jnp
from jax import lax
from jax.experimental import pallas as pl
from jax.experimental.pallas import tpu as pltpu

F32 = jnp.float32
BF16 = jnp.bfloat16

D_MODEL = 1024
HEAD_DIM = 64
NSA_HEADS = 8
NSA_GROUPS = 2
NSA_HPG = NSA_HEADS // NSA_GROUPS
NSA_WIDTH = NSA_HEADS * HEAD_DIM
KV_WIDTH = NSA_GROUPS * HEAD_DIM
CMP_BLOCK = 32
CMP_STRIDE = 16
CMP_HIDDEN = 256
SLC_BLOCK = 64
SLC_TOPK = 16
WINDOW = 512
FORCE = 1e6
CONV_WIDTH = 512
CONV_KSIZE = 31
MEM_HEADS = 4
MEM_HEAD_DIM = 128
MEM_WIDTH = MEM_HEADS * MEM_HEAD_DIM
N_BRANCH = 3
D_FF = 2816
RMS_EPS = 1e-6
LN_EPS = 1e-5

LANES = 128
F32_SUBLANES = 8
VMEM_LIMIT = 56 * 1024 * 1024

TM_PROJ = 512
TQ = 512
TK = 1024
WSUB = 128
NSA_CHAINS = 1
LOG2E = 1.4426950408889634
TT_CONV = 512
CONV_HALO = 32
CONV_ROWS = 64
TM_MERGE = 512
TM_FFN = 512
FF_CHUNK = D_FF // 2
MASKED = -1e30
SEL_BIAS = -1e9
SEL_PAD = TK - TQ
W_ONE = HEAD_DIM + 1


def _dot(a, b):
    return jnp.dot(a, b, preferred_element_type=F32)


def _dot_nt(a, b):
    return lax.dot_general(a, b, (((1,), (1,)), ((), ())), preferred_element_type=F32)


def _head_norm(t, bd, gain):
    ms = _dot((t * t).astype(BF16), bd)
    return t * lax.rsqrt(ms + RMS_EPS) * gain


def _sigmoid(x):
    return 1.0 / (1.0 + jnp.exp(-x))


def _const_spec(shape):
    nd = len(shape)
    return pl.BlockSpec(shape, lambda *_: (0,) * nd, pipeline_mode=pl.Buffered(1))


def _params(*sem):
    return pltpu.CompilerParams(dimension_semantics=sem, vmem_limit_bytes=VMEM_LIMIT)


def _inproj_kernel(x_ref, g_ref, wq_ref, wkv_ref, wg_ref, wc_ref, wm_ref, wgm_ref,
                   bdq_ref, bdk_ref, bdm_ref, gq_ref, gks_ref, gkw_ref, gqm_ref,
                   q_ref, kc_ref, vc_ref, ks_ref, vs_ref, kw_ref, vw_ref, gn_ref, u_ref, qm_ref, gm_ref,
                   *, tiles_per_seq):
    tm = x_ref.shape[0]
    n_sel = ks_ref.shape[3] - LANES
    x = x_ref[...]
    ms = jnp.mean(x * x, axis=-1, keepdims=True)
    xn = (x * lax.rsqrt(ms + RMS_EPS) * g_ref[...]).astype(BF16)

    qn = _head_norm(_dot(xn, wq_ref[...]), bdq_ref[...], gq_ref[...]).astype(BF16)
    for h in range(NSA_HEADS):
        q_ref[0, h] = qn[:, h * HEAD_DIM:(h + 1) * HEAD_DIM]

    pkv = _dot(xn, wkv_ref[...])
    k_c = pkv[:, 0 * KV_WIDTH:1 * KV_WIDTH]
    v_c = pkv[:, 1 * KV_WIDTH:2 * KV_WIDTH]
    k_s = _head_norm(pkv[:, 2 * KV_WIDTH:3 * KV_WIDTH], bdk_ref[...], gks_ref[...]).astype(BF16)
    v_s = pkv[:, 3 * KV_WIDTH:4 * KV_WIDTH].astype(BF16)
    k_w = _head_norm(pkv[:, 4 * KV_WIDTH:5 * KV_WIDTH], bdk_ref[...], gkw_ref[...]).astype(BF16)
    v_w = pkv[:, 5 * KV_WIDTH:6 * KV_WIDTH].astype(BF16)
    gates = _sigmoid(_dot(xn, wg_ref[...])).astype(BF16)

    t0 = (pl.program_id(0) % tiles_per_seq) * tm
    blk_of_row = (t0 + lax.broadcasted_iota(jnp.int32, (tm, n_sel), 0)) // SLC_BLOCK
    onehot = jnp.where(blk_of_row == lax.broadcasted_iota(jnp.int32, (tm, n_sel), 1), 1.0, 0.0).astype(BF16)
    rest = LANES - HEAD_DIM
    zeros = jnp.zeros((tm, rest), BF16)
    rest_lane = lax.broadcasted_iota(jnp.int32, (tm, rest), 1)
    ones_sel = jnp.where(rest_lane == 0, 1.0, 0.0).astype(BF16)
    ones_win = jnp.where(rest_lane == W_ONE - HEAD_DIM, 1.0, 0.0).astype(BF16)
    for g in range(NSA_GROUPS):
        c = slice(g * HEAD_DIM, (g + 1) * HEAD_DIM)
        kc_ref[0, g] = k_c[:, c]
        vc_ref[0, g] = v_c[:, c]
        ks_ref[0, g, :, 0:HEAD_DIM] = k_s[:, c]
        ks_ref[0, g, :, HEAD_DIM:LANES] = zeros
        ks_ref[0, g, :, LANES:] = onehot
        vs_ref[0, g, :, 0:HEAD_DIM] = v_s[:, c]
        vs_ref[0, g, :, HEAD_DIM:] = ones_sel
        kw_ref[0, g, :, 0:HEAD_DIM] = k_w[:, c]
        kw_ref[0, g, :, HEAD_DIM:] = zeros
        vw_ref[0, g, :, 0:HEAD_DIM] = v_w[:, c]
        vw_ref[0, g, :, HEAD_DIM:] = ones_win
        gn_ref[0, g] = gates[:, g * LANES:(g + 1) * LANES]

    pc = _dot(xn, wc_ref[...])
    u_ref[...] = pc[:, :CONV_WIDTH] * _sigmoid(pc[:, CONV_WIDTH:])

    qm_ref[...] = _head_norm(_dot(xn, wm_ref[...]), bdm_ref[...], gqm_ref[...]).astype(BF16)

    gm_ref[...] = _sigmoid(_dot(xn, wgm_ref[...])).astype(BF16)


def _block_diag(n, blk):
    idx = np.arange(n) // blk
    return jnp.asarray((idx[:, None] == idx[None, :]).astype(np.float32) / blk, dtype=BF16)


def _inproj(x, norm_mix, w_in, nsa_qk_norm, mem_qk_norm):
    b, s, _ = x.shape
    m = b * s
    tm = min(TM_PROJ, s)
    per_seq = s // tm
    n_sel = s // SLC_BLOCK
    sizes = [NSA_WIDTH, 6 * KV_WIDTH, 3 * NSA_HEADS, 2 * CONV_WIDTH, MEM_WIDTH, N_BRANCH * D_MODEL]
    pts = np.cumsum(sizes)[:-1]
    wq, wkv, wg, wc, wm, wgm = [w.astype(BF16) for w in jnp.split(w_in, pts, axis=1)]
    per_group = 3 * NSA_HPG
    wg = jnp.pad(wg.reshape(D_MODEL, NSA_GROUPS, per_group), ((0, 0), (0, 0), (0, LANES - per_group)))
    wg = wg.reshape(D_MODEL, NSA_GROUPS * LANES)
    gq = (jnp.tile(nsa_qk_norm[0], NSA_HEADS) * (HEAD_DIM ** -0.5 * LOG2E))[None, :]
    gks = jnp.tile(nsa_qk_norm[2], NSA_GROUPS)[None, :]
    gkw = jnp.tile(nsa_qk_norm[3], NSA_GROUPS)[None, :]
    gqm = (jnp.tile(mem_qk_norm[0], MEM_HEADS) * (MEM_HEAD_DIM ** -0.5))[None, :]
    bdq = _block_diag(NSA_WIDTH, HEAD_DIM)
    bdk = _block_diag(KV_WIDTH, HEAD_DIM)
    bdm = _block_diag(MEM_WIDTH, MEM_HEAD_DIM)
    consts = [norm_mix[None, :], wq, wkv, wg, wc, wm, wgm, bdq, bdk, bdm, gq, gks, gkw, gqm]

    def row(n):
        return pl.BlockSpec((tm, n), lambda i: (i, 0))

    def heads(n_heads, width):
        return pl.BlockSpec((1, n_heads, tm, width), lambda i: (i // per_seq, 0, i % per_seq, 0))

    grp = NSA_GROUPS
    outs = [((b, NSA_HEADS, s, HEAD_DIM), BF16, heads(NSA_HEADS, HEAD_DIM)),
            ((b, grp, s, HEAD_DIM), F32, heads(grp, HEAD_DIM)),
            ((b, grp, s, HEAD_DIM), F32, heads(grp, HEAD_DIM)),
            ((b, grp, s, LANES + n_sel), BF16, heads(grp, LANES + n_sel)),
            ((b, grp, s, LANES), BF16, heads(grp, LANES)),
            ((b, grp, s, LANES), BF16, heads(grp, LANES)),
            ((b, grp, s, LANES), BF16, heads(grp, LANES)),
            ((b, grp, s, LANES), BF16, heads(grp, LANES)),
            ((m, CONV_WIDTH), F32, row(CONV_WIDTH)),
            ((m, MEM_WIDTH), BF16, row(MEM_WIDTH)),
            ((m, N_BRANCH * D_MODEL), BF16, row(N_BRANCH * D_MODEL))]
    return pl.pallas_call(
        functools.partial(_inproj_kernel, tiles_per_seq=per_seq),
        grid=(m // tm,),
        in_specs=[row(D_MODEL)] + [_const_spec(c.shape) for c in consts],
        out_specs=[spec for _, _, spec in outs],
        out_shape=[jax.ShapeDtypeStruct(shape, dt) for shape, dt, _ in outs],
        compiler_params=_params("parallel"),
        name="inproj",
    )(x.reshape(m, D_MODEL), *consts)


def _gelu_tanh(x):
    return 0.5 * x * (1.0 + jnp.tanh(np.sqrt(2.0 / np.pi) * (x + 0.044715 * (x * x * x))))


def _compress_kernel(x_ref, plo_ref, phi_ref, w1a_ref, w1b_ref, w2_ref, gain_ref, o_ref, *, normalize):
    x = x_ref[0, 0]
    a = _dot((x + plo_ref[...]).astype(BF16), w1a_ref[...])
    b = _dot((x + phi_ref[...]).astype(BF16), w1b_ref[...])
    n = x.shape[0]
    h = a + pltpu.roll(b, n - 1, 0)
    y = _dot(_gelu_tanh(h).astype(BF16), w2_ref[...])
    if normalize:
        ms = jnp.mean(y * y, axis=-1, keepdims=True)
        y = y * lax.rsqrt(ms + RMS_EPS) * gain_ref[...]
    o_ref[0, 0] = y.astype(o_ref.dtype)


def _compress(xc, pos, w1, w2, gain, normalize):
    b, g, n, width = xc.shape
    half = CMP_STRIDE * HEAD_DIM
    plo = pos[:CMP_STRIDE].reshape(1, half)
    phi = pos[CMP_STRIDE:].reshape(1, half)
    w1a = w1[:half].astype(BF16)
    w1b = w1[half:].astype(BF16)
    consts = [plo, phi, w1a, w1b, w2.astype(BF16), gain[None, :]]
    return pl.pallas_call(
        functools.partial(_compress_kernel, normalize=normalize),
        grid=(b, g),
        in_specs=[pl.BlockSpec((1, 1, n, width), lambda i, j: (i, j, 0, 0))] + [_const_spec(c.shape) for c in consts],
        out_specs=pl.BlockSpec((1, 1, n, HEAD_DIM), lambda i, j: (i, j, 0, 0)),
        out_shape=jax.ShapeDtypeStruct((b, g, n, HEAD_DIM), BF16),
        compiler_params=_params("parallel", "parallel"),
        name="compress_k" if normalize else "compress_v",
    )(xc, *consts)


def _nsa_kernel(q_ref, kc_ref, vc_ref, ks_ref, vs_ref, kw_ref, vw_ref, gate_ref, mt_ref, ex_ref, o_ref,
                qaug_scr, m_scr, acc_scr, *, n_cmp, n_sel):
    tq = q_ref.shape[2]
    hpc = NSA_HPG // NSA_CHAINS
    crow = hpc * tq
    chains = range(NSA_CHAINS)
    i = pl.program_id(2)
    q0 = i * tq

    def q_chain(r):
        return q_ref[0, r * hpc:(r + 1) * hpc].reshape(crow, HEAD_DIM)

    def rows_of(r):
        return slice(r * crow, (r + 1) * crow)

    def stack(a, n=hpc):
        return jnp.concatenate([a] * n, axis=0)

    def causal_bias(n, keep):
        r_i = lax.broadcasted_iota(jnp.int32, (n, n), 0)
        c_i = lax.broadcasted_iota(jnp.int32, (n, n), 1)
        return jnp.where(keep(r_i, c_i), 0.0, MASKED)

    t_col = q0 + lax.broadcasted_iota(jnp.int32, (tq, 1), 0)

    rest = LANES - HEAD_DIM
    flag_col = jnp.where(lax.broadcasted_iota(jnp.int32, (crow, rest), 1) == 0, SEL_BIAS, 0.0).astype(BF16)
    for r in chains:
        qaug_scr[rows_of(r), 0:HEAD_DIM] = q_chain(r)
        qaug_scr[rows_of(r), HEAD_DIM:LANES] = flag_col

    cmp_end = lax.broadcasted_iota(jnp.int32, (1, n_cmp), 1) * CMP_STRIDE + (CMP_BLOCK - 1)
    cbias = stack(jnp.where(cmp_end <= t_col, 0.0, MASKED))
    seen = stack(jnp.where(t_col >= CMP_BLOCK - 1, 1.0, 0.0))
    o_c = []
    psum = None
    for r in chains:
        s = _dot_nt(q_chain(r), kc_ref[0, 0]) + cbias
        e = jnp.exp2(s - jnp.max(s, axis=-1, keepdims=True))
        p = e * (seen / jnp.sum(e, axis=-1, keepdims=True))
        o_c.append(_dot(p.astype(BF16), vc_ref[0, 0]))
        for h in range(hpc):
            ph = p[h * tq:(h + 1) * tq]
            psum = ph if psum is None else psum + ph

    hi = psum.astype(BF16)
    lo = (psum - hi.astype(F32)).astype(BF16)
    imp = _dot_nt(mt_ref[...], hi) + _dot_nt(mt_ref[...], lo)
    blk = lax.broadcasted_iota(jnp.int32, (n_sel, tq), 0)
    t_row = q0 + lax.broadcasted_iota(jnp.int32, (n_sel, tq), 1)
    cur = t_row // SLC_BLOCK
    forced = (blk == 0) | (blk == cur) | (blk == cur - 1)
    future = blk * SLC_BLOCK > t_row
    n_forced = 3
    v = jnp.where(forced, -jnp.inf, jnp.where(future, -FORCE, imp))
    blk_f = blk.astype(F32)

    def extract(v, n):
        for _ in range(n):
            top = jnp.max(v, axis=0, keepdims=True)
            first = jnp.min(jnp.where(v == top, blk_f, float(n_sel)), axis=0, keepdims=True)
            v = jnp.where(blk_f == first, -jnp.inf, v)
        return v

    span = WINDOW + WSUB
    pair = 2
    pieces = [(sub, hp) for sub in range(tq // WSUB) for hp in range(NSA_HPG // pair)]
    starts = {sub: pl.multiple_of(q0 + sub * WSUB, WSUB) for sub in range(tq // WSUB)}
    win_scores = {}
    for sub, hp in pieces:
        qw = jnp.concatenate([qaug_scr[h * tq + sub * WSUB:h * tq + (sub + 1) * WSUB, 0:LANES]
                              for h in range(hp * pair, (hp + 1) * pair)], axis=0)
        win_scores[sub, hp] = _dot_nt(qw, kw_ref[0, 0, pl.ds(starts[sub], span), :])
    enter = stack(causal_bias(WSUB, lambda r_i, c_i: c_i > r_i), pair)
    leave = stack(causal_bias(WSUB, lambda r_i, c_i: c_i <= r_i), pair)
    todo = min(SLC_TOPK, n_sel) - n_forced
    per_piece = -(-todo // len(pieces))
    o_w = {}
    for sub, hp in pieces:
        sw = win_scores[sub, hp]
        sw = jnp.concatenate([sw[:, :WSUB] + enter, sw[:, WSUB:WINDOW], sw[:, WINDOW:] + leave], axis=1)
        pw = jnp.exp2(sw - jnp.max(sw, axis=-1, keepdims=True))
        o_w[sub, hp] = _dot(pw.astype(BF16), vw_ref[0, 0, pl.ds(starts[sub], span), :])
        v = extract(v, min(per_piece, todo))
        todo -= min(per_piece, todo)
    bias_t = jnp.where(future, SEL_BIAS, jnp.where(v == -jnp.inf, 0.0, SEL_BIAS))
    bias = stack(jnp.transpose(bias_t).astype(BF16))

    for r in chains:
        qaug_scr[rows_of(r), LANES:LANES + n_sel] = bias
    m_scr[...] = jnp.full(m_scr.shape, MASKED, F32)
    acc_scr[...] = jnp.zeros(acc_scr.shape, F32)

    def sel_chunk(k0, tail_bias):
        for r in chains:
            sc = _dot_nt(qaug_scr[rows_of(r)], ks_ref[0, 0, pl.ds(k0, TK), :])
            if tail_bias is not None:
                sc = jnp.concatenate([sc[:, :TK - tq], sc[:, TK - tq:] + tail_bias], axis=1)
            m_old = m_scr[rows_of(r)]
            m_new = jnp.maximum(m_old, jnp.max(sc, axis=-1, keepdims=True))
            pc = jnp.exp2(sc - jnp.concatenate([m_new] * (TK // LANES), axis=1))
            acc_scr[rows_of(r)] = (jnp.exp2(m_old - m_new) * acc_scr[rows_of(r)]
                                   + _dot(pc.astype(BF16), vs_ref[0, 0, pl.ds(k0, TK), :]))
            m_scr[rows_of(r)] = m_new

    n_low = q0 // TK
    low0 = q0 - n_low * TK

    def body(c, carry):
        sel_chunk(pl.multiple_of(low0 + c * TK, tq), None)
        return carry

    lax.fori_loop(0, n_low, body, 0)
    sel_chunk(pl.multiple_of(q0, tq), stack(causal_bias(tq, lambda r_i, c_i: c_i <= r_i)))

    lane = lax.broadcasted_iota(jnp.int32, (tq, LANES), 1)
    gates = gate_ref[0, 0].astype(F32)
    for r in chains:
        acc = acc_scr[rows_of(r)]
        for hh in range(hpc):
            h = r * hpc + hh
            rs = slice(hh * tq, (hh + 1) * tq)
            a_s = acc[rs]
            a_w = jnp.concatenate([o_w[sub, h // pair][(h % pair) * WSUB:(h % pair + 1) * WSUB]
                                   for sub in range(tq // WSUB)], axis=0)
            scal = jnp.where(lane == HEAD_DIM, 1.0 / a_s, jnp.where(lane == W_ONE, 1.0 / a_w, gates))
            y = _dot(scal.astype(BF16), ex_ref[h])

            def slot(k):
                return y[:, k * LANES:k * LANES + HEAD_DIM]

            out = (slot(0) * o_c[r][rs] + (slot(1) * slot(3)) * a_s[:, :HEAD_DIM]
                   + (slot(2) * slot(4)) * a_w[:, :HEAD_DIM])
            o_ref[0, :, h * HEAD_DIM:(h + 1) * HEAD_DIM] = out.astype(o_ref.dtype)


def _importance_matrix(n_cmp, n_sel):
    ratio = SLC_BLOCK // CMP_STRIDE
    c = np.arange(n_cmp)[None, :]
    j = np.arange(n_sel)[:, None]
    return jnp.asarray(((c >= ratio * j - 1) & (c <= ratio * j + ratio - 1)).astype(np.float32), dtype=BF16)


def _spread_matrix():
    e = np.zeros((NSA_HPG, LANES, 5 * LANES), np.float32)
    for h in range(NSA_HPG):
        for k, src in enumerate([3 * h, 3 * h + 1, 3 * h + 2, HEAD_DIM, W_ONE]):
            e[h, src, k * LANES:k * LANES + HEAD_DIM] = 1.0
    return jnp.asarray(e, dtype=BF16)


def _nsa(q, kc, vc, k_aug, vs_aug, kw_aug, vw_aug, gate):
    b, _, s, _ = q.shape
    tq = min(TQ, s)
    n_cmp = s // CMP_STRIDE
    n_sel = s // SLC_BLOCK
    assert n_sel <= LANES and s % TK == 0 and TK % tq == 0 and tq % WSUB == 0 and SEL_PAD == TK - tq

    def left_pad(t, rows, flagged):
        pad = jnp.zeros((rows, t.shape[-1]), BF16)
        if flagged:
            pad = pad.at[:, HEAD_DIM].set(1.0)
        return jnp.concatenate([jnp.broadcast_to(pad, t.shape[:2] + pad.shape), t], axis=2)

    ks_p = left_pad(k_aug, SEL_PAD, True)
    vs_p = left_pad(vs_aug, SEL_PAD, False)
    kw_p = left_pad(kw_aug, WINDOW, True)
    vw_p = left_pad(vw_aug, WINDOW, False)
    mt = _importance_matrix(n_cmp, n_sel)
    ex = _spread_matrix()
    rows = NSA_HPG * tq

    def per_group(shape):
        return pl.BlockSpec((1, 1) + shape, lambda bi, gi, i: (bi, gi, 0, 0))

    return pl.pallas_call(
        functools.partial(_nsa_kernel, n_cmp=n_cmp, n_sel=n_sel),
        grid=(b, NSA_GROUPS, s // tq),
        in_specs=[
            pl.BlockSpec((1, NSA_HPG, tq, HEAD_DIM), lambda bi, gi, i: (bi, gi, i, 0)),
            per_group((n_cmp, HEAD_DIM)), per_group((n_cmp, HEAD_DIM)),
            per_group((s + SEL_PAD, LANES + n_sel)), per_group((s + SEL_PAD, LANES)),
            per_group((s + WINDOW, LANES)), per_group((s + WINDOW, LANES)),
            pl.BlockSpec((1, 1, tq, LANES), lambda bi, gi, i: (bi, gi, i, 0)),
            _const_spec(mt.shape), _const_spec(ex.shape),
        ],
        out_specs=pl.BlockSpec((1, tq, NSA_HPG * HEAD_DIM), lambda bi, gi, i: (bi, i, gi)),
        out_shape=jax.ShapeDtypeStruct((b, s, NSA_WIDTH), BF16),
        scratch_shapes=[pltpu.VMEM((rows, LANES + n_sel), BF16),
                        pltpu.VMEM((rows, LANES), F32),
                        pltpu.VMEM((rows, LANES), F32)],
        compiler_params=_params("parallel", "parallel", "arbitrary"),
        name="nsa",
    )(q, kc, vc, ks_p, vs_p, kw_p, vw_p, gate, mt, ex)


def _conv_kernel(prev_ref, cur_ref, w_ref, b_ref, g_ref, beta_ref, o_ref, buf, shifted):
    tt = cur_ref.shape[1]
    i = pl.program_id(1)
    hist = prev_ref[0]
    buf[0:CONV_HALO] = jnp.where(i > 0, hist, jnp.zeros_like(hist))
    buf[CONV_HALO:] = cur_ref[0]
    off = CONV_HALO - (CONV_KSIZE - 1)
    for rho in range(1, F32_SUBLANES):
        shifted[rho - 1] = buf[rho:rho + shifted.shape[1]]
    for r0 in range(0, tt, CONV_ROWS):
        acc = jnp.zeros((CONV_ROWS, CONV_WIDTH), F32) + b_ref[...]
        for k in range(CONV_KSIZE):
            rho = (off + k) % F32_SUBLANES
            a = r0 + off + k - rho
            src = buf[a:a + CONV_ROWS] if rho == 0 else shifted[rho - 1, a:a + CONV_ROWS]
            acc = acc + src * w_ref[k:k + 1]
        mu = jnp.mean(acc, axis=-1, keepdims=True)
        d = acc - mu
        var = jnp.mean(d * d, axis=-1, keepdims=True)
        y = d * lax.rsqrt(var + LN_EPS) * g_ref[...] + beta_ref[...]
        o_ref[0, r0:r0 + CONV_ROWS] = (y * _sigmoid(y)).astype(o_ref.dtype)


def _conv_module(u, conv_w, conv_b, ln_g, ln_b):
    b, s, _ = u.shape
    tt = min(TT_CONV, s)
    per = tt // CONV_HALO
    consts = [conv_w.reshape(CONV_KSIZE, CONV_WIDTH), conv_b[None, :], ln_g[None, :], ln_b[None, :]]
    return pl.pallas_call(
        _conv_kernel,
        grid=(b, s // tt),
        in_specs=[pl.BlockSpec((1, CONV_HALO, CONV_WIDTH), lambda bi, i: (bi, jnp.maximum(i * per - 1, 0), 0)),
                  pl.BlockSpec((1, tt, CONV_WIDTH), lambda bi, i: (bi, i, 0))]
                 + [_const_spec(c.shape) for c in consts],
        out_specs=pl.BlockSpec((1, tt, CONV_WIDTH), lambda bi, i: (bi, i, 0)),
        out_shape=jax.ShapeDtypeStruct((b, s, CONV_WIDTH), BF16),
        scratch_shapes=[pltpu.VMEM((CONV_HALO + tt, CONV_WIDTH), F32),
                        pltpu.VMEM((F32_SUBLANES - 1, tt + CONV_HALO - F32_SUBLANES, CONV_WIDTH), F32)],
        compiler_params=_params("parallel", "parallel"),
        name="conv",
    )(u, u, *consts)


def _memkv_kernel(mem_ref, g_ref, w_ref, bd_ref, gk_ref, km_ref, vm_ref):
    x = mem_ref[0]
    ms = jnp.mean(x * x, axis=-1, keepdims=True)
    xn = (x * lax.rsqrt(ms + RMS_EPS) * g_ref[...]).astype(BF16)
    kv = _dot(xn, w_ref[...])
    km_ref[0] = _head_norm(kv[:, :MEM_WIDTH], bd_ref[...], gk_ref[...]).astype(BF16)
    vm_ref[0] = kv[:, MEM_WIDTH:].astype(BF16)


def _memkv(mem, norm_mem, w_mem_kv, gain_k):
    b, n, _ = mem.shape
    consts = [norm_mem[None, :], w_mem_kv.astype(BF16), _block_diag(MEM_WIDTH, MEM_HEAD_DIM),
              jnp.tile(gain_k, MEM_HEADS)[None, :]]
    spec = pl.BlockSpec((1, n, MEM_WIDTH), lambda i: (i, 0, 0))
    return pl.pallas_call(
        _memkv_kernel,
        grid=(b,),
        in_specs=[pl.BlockSpec((1, n, D_MODEL), lambda i: (i, 0, 0))] + [_const_spec(c.shape) for c in consts],
        out_specs=[spec, spec],
        out_shape=[jax.ShapeDtypeStruct((b, n, MEM_WIDTH), BF16)] * 2,
        compiler_params=_params("parallel"),
        name="memkv",
    )(mem, *consts)


def _merge_kernel(x_ref, onsa_ref, uconv_ref, qm_ref, gm_ref, km_ref, vm_ref,
                  wn_ref, wc_ref, wmo_ref, wo_ref, h_ref):
    qm = qm_ref[...]
    km = km_ref[0]
    vm = vm_ref[0]
    heads = []
    for h in range(MEM_HEADS):
        c = slice(h * MEM_HEAD_DIM, (h + 1) * MEM_HEAD_DIM)
        s = _dot_nt(qm[:, c], km[:, c])
        e = jnp.exp(s - jnp.max(s, axis=-1, keepdims=True))
        p = e * (1.0 / jnp.sum(e, axis=-1, keepdims=True))
        heads.append(_dot(p.astype(BF16), vm[:, c]))
    o_m = jnp.concatenate(heads, axis=1).astype(BF16)
    merged = (gm_ref[:, 0:D_MODEL].astype(F32) * _dot(onsa_ref[...], wn_ref[...])
              + gm_ref[:, D_MODEL:2 * D_MODEL].astype(F32) * _dot(uconv_ref[...], wc_ref[...])
              + gm_ref[:, 2 * D_MODEL:3 * D_MODEL].astype(F32) * _dot(o_m, wmo_ref[...]))
    h_ref[...] = x_ref[...] + _dot(merged.astype(BF16), wo_ref[...])


def _merge(x2, o_nsa, u_conv, qm, gm, km, vm, w_nsa_out, w_conv_out, w_mem_out, w_out, seq):
    m = x2.shape[0]
    tm = min(TM_MERGE, seq)
    per_b = seq // tm
    n_mem = km.shape[1]
    consts = [w_nsa_out.astype(BF16), w_conv_out.astype(BF16), w_mem_out.astype(BF16), w_out.astype(BF16)]

    def row(n):
        return pl.BlockSpec((tm, n), lambda i: (i, 0))

    mem_spec = pl.BlockSpec((1, n_mem, MEM_WIDTH), lambda i: (i // per_b, 0, 0))
    return pl.pallas_call(
        _merge_kernel,
        grid=(m // tm,),
        in_specs=[row(D_MODEL), row(NSA_WIDTH), row(CONV_WIDTH), row(MEM_WIDTH), row(N_BRANCH * D_MODEL),
                  mem_spec, mem_spec] + [_const_spec(c.shape) for c in consts],
        out_specs=row(D_MODEL),
        out_shape=jax.ShapeDtypeStruct((m, D_MODEL), F32),
        compiler_params=_params("parallel"),
        name="merge",
    )(x2, o_nsa, u_conv, qm, gm, km, vm, *consts)


def _ffn_kernel(h_ref, g_ref, wg_ref, wu_ref, wd_ref, o_ref):
    h = h_ref[...]
    ms = jnp.mean(h * h, axis=-1, keepdims=True)
    hn = (h * lax.rsqrt(ms + RMS_EPS) * g_ref[...]).astype(BF16)
    acc = h
    for c in range(0, D_FF, FF_CHUNK):
        gate = _dot(hn, wg_ref[:, c:c + FF_CHUNK])
        up = _dot(hn, wu_ref[:, c:c + FF_CHUNK])
        act = (gate * _sigmoid(gate) * up).astype(BF16)
        acc = acc + _dot(act, wd_ref[c:c + FF_CHUNK, :])
    o_ref[...] = acc


def _ffn(h, norm_ffn, w_gate, w_up, w_down):
    m = h.shape[0]
    tm = min(TM_FFN, m)
    consts = [norm_ffn[None, :], w_gate.astype(BF16), w_up.astype(BF16), w_down.astype(BF16)]
    row = pl.BlockSpec((tm, D_MODEL), lambda i: (i, 0))
    return pl.pallas_call(
        _ffn_kernel,
        grid=(m // tm,),
        in_specs=[row] + [_const_spec(c.shape) for c in consts],
        out_specs=row,
        out_shape=jax.ShapeDtypeStruct((m, D_MODEL), F32),
        compiler_params=_params("parallel"),
        name="ffn",
    )(h, *consts)


def _layer(x, mem, norm_mix, w_in, nsa_qk_norm, cmp_pos, cmp_w1, cmp_w2, w_nsa_out,
           conv_w, conv_b, conv_ln_g, conv_ln_b, w_conv_out, norm_mem, w_mem_kv,
           mem_qk_norm, w_mem_out, w_out, norm_ffn, w_gate, w_up, w_down):
    b, s, _ = x.shape
    x2 = x.reshape(b * s, D_MODEL)
    q, kc_raw, vc_raw, k_aug, vs_aug, kw_aug, vw_aug, gate, u, qm, gm = _inproj(
        x, norm_mix, w_in, nsa_qk_norm, mem_qk_norm)

    def chunked(t):
        return t.reshape(b, NSA_GROUPS, s // CMP_STRIDE, CMP_STRIDE * HEAD_DIM)

    kc = _compress(chunked(kc_raw), cmp_pos[0], cmp_w1[0], cmp_w2[0], nsa_qk_norm[1], True)
    vc = _compress(chunked(vc_raw), cmp_pos[1], cmp_w1[1], cmp_w2[1], nsa_qk_norm[1], False)
    o_nsa = _nsa(q, kc, vc, k_aug, vs_aug, kw_aug, vw_aug, gate)

    u_conv = _conv_module(u.reshape(b, s, CONV_WIDTH), conv_w, conv_b, conv_ln_g, conv_ln_b)
    km, vm = _memkv(mem, norm_mem, w_mem_kv, mem_qk_norm[1])
    h = _merge(x2, o_nsa.reshape(b * s, NSA_WIDTH), u_conv.reshape(b * s, CONV_WIDTH), qm, gm, km, vm,
               w_nsa_out, w_conv_out, w_mem_out, w_out, s)
    out = _ffn(h, norm_ffn, w_gate, w_up, w_down)
    return out.reshape(b, s, D_MODEL)


def kernel(x, mem, norm_mix, w_in, nsa_qk_norm, cmp_pos, cmp_w1, cmp_w2, w_nsa_out, conv_w, conv_b, conv_ln_g, conv_ln_b, w_conv_out, norm_mem, w_mem_kv, mem_qk_norm, w_mem_out, w_out, norm_ffn, w_gate, w_up, w_down):
    h = x
    for l in range(norm_mix.shape[0]):
        h = _layer(h, mem, norm_mix[l], w_in[l], nsa_qk_norm[l], cmp_pos[l], cmp_w1[l], cmp_w2[l],
                   w_nsa_out[l], conv_w[l], conv_b[l], conv_ln_g[l], conv_ln_b[l], w_conv_out[l],
                   norm_mem[l], w_mem_kv[l], mem_qk_norm[l], w_mem_out[l], w_out[l], norm_ffn[l],
                   w_gate[l], w_up[l], w_down[l])
    return h
```

```python
import functools

import numpy as np
import jax
import jax.numpy as jnp
from jax import lax
from jax.experimental import pallas as pl
from jax.experimental.pallas import tpu as pltpu

F32 = jnp.float32
BF16 = jnp.bfloat16

D_MODEL = 1024
HEAD_DIM = 64
NSA_HEADS = 8
NSA_GROUPS = 2
NSA_HPG = NSA_HEADS // NSA_GROUPS
NSA_WIDTH = NSA_HEADS * HEAD_DIM
KV_WIDTH = NSA_GROUPS * HEAD_DIM
CMP_BLOCK = 32
CMP_STRIDE = 16
CMP_HIDDEN = 256
SLC_BLOCK = 64
SLC_TOPK = 16
WINDOW = 512
FORCE = 1e6
CONV_WIDTH = 512
CONV_KSIZE = 31
MEM_HEADS = 4
MEM_HEAD_DIM = 128
MEM_WIDTH = MEM_HEADS * MEM_HEAD_DIM
N_BRANCH = 3
D_FF = 2816
RMS_EPS = 1e-6
LN_EPS = 1e-5

LANES = 128
F32_SUBLANES = 8
VMEM_LIMIT = 56 * 1024 * 1024

TM_PROJ = 512
TQ = 512
TK = 1024
WSUB = 128
NSA_CHAINS = 1
LOG2E = 1.4426950408889634
CONV_HALO = 32
CONV_ROWS = 64
TM_MERGE = 512
TM_FFN = 512
FF_CHUNK = D_FF // 2
MASKED = -1e30
SEL_BIAS = -1e9
SEL_PAD = TK - TQ
W_ONE = HEAD_DIM + 1


def _dot(a, b):
    return jnp.dot(a, b, preferred_element_type=F32)


def _dot_nt(a, b):
    return lax.dot_general(a, b, (((1,), (1,)), ((), ())), preferred_element_type=F32)


def _head_norm(t, bd, gain):
    ms = _dot((t * t).astype(BF16), bd)
    return t * lax.rsqrt(ms + RMS_EPS) * gain


def _sigmoid(x):
    return 1.0 / (1.0 + jnp.exp(-x))


def _const_spec(shape):
    nd = len(shape)
    return pl.BlockSpec(shape, lambda *_: (0,) * nd, pipeline_mode=pl.Buffered(1))


def _params(*sem):
    return pltpu.CompilerParams(dimension_semantics=sem, vmem_limit_bytes=VMEM_LIMIT)


def _conv_prepare(buf, shifted):
    for rho in range(1, F32_SUBLANES):
        shifted[rho - 1] = buf[rho:rho + shifted.shape[1]]


def _conv_rows(buf, shifted, w_ref, b_ref, g_ref, beta_ref, o_ref, r0):
    off = CONV_HALO - (CONV_KSIZE - 1)
    acc = jnp.zeros((CONV_ROWS, CONV_WIDTH), F32) + b_ref[...]
    for k in range(CONV_KSIZE):
        rho = (off + k) % F32_SUBLANES
        a = r0 + off + k - rho
        src = buf[a:a + CONV_ROWS] if rho == 0 else shifted[rho - 1, a:a + CONV_ROWS]
        acc = acc + src * w_ref[k:k + 1]
    mu = jnp.mean(acc, axis=-1, keepdims=True)
    d = acc - mu
    var = jnp.mean(d * d, axis=-1, keepdims=True)
    y = d * lax.rsqrt(var + LN_EPS) * g_ref[...] + beta_ref[...]
    o_ref[r0:r0 + CONV_ROWS] = (y * _sigmoid(y)).astype(o_ref.dtype)


def _inproj_kernel(x_ref, g_ref, wq_ref, wkv_ref, wg_ref, wc_ref, wm_ref, wgm_ref,
                   bdq_ref, bdk_ref, bdm_ref, gq_ref, gks_ref, gkw_ref, gqm_ref, cw_ref, cb_ref, lg_ref, lb_ref,
                   q_ref, kc_ref, vc_ref, ks_ref, vs_ref, kw_ref, vw_ref, gn_ref, u_ref, qm_ref, gm_ref,
                   buf, shifted, *, tiles_per_seq):
    tm = x_ref.shape[0]
    n_sel = ks_ref.shape[3] - LANES

    first = pl.program_id(0) % tiles_per_seq == 0

    @pl.when(first)
    def _():
        buf[0:CONV_HALO] = jnp.zeros((CONV_HALO, CONV_WIDTH), F32)

    @pl.when(jnp.logical_not(first))
    def _():
        buf[0:CONV_HALO] = buf[tm:tm + CONV_HALO]

    x = x_ref[...]
    ms = jnp.mean(x * x, axis=-1, keepdims=True)
    xn = (x * lax.rsqrt(ms + RMS_EPS) * g_ref[...]).astype(BF16)

    pc = _dot(xn, wc_ref[...])
    buf[CONV_HALO:] = pc[:, :CONV_WIDTH] * _sigmoid(pc[:, CONV_WIDTH:])
    _conv_prepare(buf, shifted)

    def seg_q():
        qn = _head_norm(_dot(xn, wq_ref[...]), bdq_ref[...], gq_ref[...]).astype(BF16)
        for h in range(NSA_HEADS):
            q_ref[0, h] = qn[:, h * HEAD_DIM:(h + 1) * HEAD_DIM]

    def seg_kv():
        pkv = _dot(xn, wkv_ref[...])
        k_c = pkv[:, 0 * KV_WIDTH:1 * KV_WIDTH]
        v_c = pkv[:, 1 * KV_WIDTH:2 * KV_WIDTH]
        k_s = _head_norm(pkv[:, 2 * KV_WIDTH:3 * KV_WIDTH], bdk_ref[...], gks_ref[...]).astype(BF16)
        v_s = pkv[:, 3 * KV_WIDTH:4 * KV_WIDTH].astype(BF16)
        k_w = _head_norm(pkv[:, 4 * KV_WIDTH:5 * KV_WIDTH], bdk_ref[...], gkw_ref[...]).astype(BF16)
        v_w = pkv[:, 5 * KV_WIDTH:6 * KV_WIDTH].astype(BF16)
        gates = _sigmoid(_dot(xn, wg_ref[...])).astype(BF16)

        t0 = (pl.program_id(0) % tiles_per_seq) * tm
        blk_of_row = (t0 + lax.broadcasted_iota(jnp.int32, (tm, n_sel), 0)) // SLC_BLOCK
        onehot = jnp.where(blk_of_row == lax.broadcasted_iota(jnp.int32, (tm, n_sel), 1), 1.0, 0.0).astype(BF16)
        rest = LANES - HEAD_DIM
        zeros = jnp.zeros((tm, rest), BF16)
        rest_lane = lax.broadcasted_iota(jnp.int32, (tm, rest), 1)
        ones_sel = jnp.where(rest_lane == 0, 1.0, 0.0).astype(BF16)
        ones_win = jnp.where(rest_lane == W_ONE - HEAD_DIM, 1.0, 0.0).astype(BF16)
        for g in range(NSA_GROUPS):
            c = slice(g * HEAD_DIM, (g + 1) * HEAD_DIM)
            kc_ref[0, g] = k_c[:, c]
            vc_ref[0, g] = v_c[:, c]
            ks_ref[0, g, :, 0:HEAD_DIM] = k_s[:, c]
            ks_ref[0, g, :, HEAD_DIM:LANES] = zeros
            ks_ref[0, g, :, LANES:] = onehot
            vs_ref[0, g, :, 0:HEAD_DIM] = v_s[:, c]
            vs_ref[0, g, :, HEAD_DIM:] = ones_sel
            kw_ref[0, g, :, 0:HEAD_DIM] = k_w[:, c]
            kw_ref[0, g, :, HEAD_DIM:] = zeros
            vw_ref[0, g, :, 0:HEAD_DIM] = v_w[:, c]
            vw_ref[0, g, :, HEAD_DIM:] = ones_win
            gn_ref[0, g] = gates[:, g * LANES:(g + 1) * LANES]

    def seg_qm():
        qm_ref[...] = _head_norm(_dot(xn, wm_ref[...]), bdm_ref[...], gqm_ref[...]).astype(BF16)

    def seg_gm(j):
        c = slice(j * D_MODEL, (j + 1) * D_MODEL)
        gm_ref[:, c] = _sigmoid(_dot(xn, wgm_ref[:, c])).astype(BF16)

    segments = [seg_q, seg_kv, seg_qm] + [functools.partial(seg_gm, j) for j in range(N_BRANCH)]
    blocks = list(range(0, tm, CONV_ROWS))
    done = 0
    for j, seg in enumerate(segments):
        seg()
        upto = (j + 1) * len(blocks) // len(segments)
        for r0 in blocks[done:upto]:
            _conv_rows(buf, shifted, cw_ref, cb_ref, lg_ref, lb_ref, u_ref, r0)
        done = upto


def _block_diag(n, blk):
    idx = np.arange(n) // blk
    return jnp.asarray((idx[:, None] == idx[None, :]).astype(np.float32) / blk, dtype=BF16)


def _inproj(x, norm_mix, w_in, nsa_qk_norm, mem_qk_norm, conv_w, conv_b, conv_ln_g, conv_ln_b):
    b, s, _ = x.shape
    m = b * s
    tm = min(TM_PROJ, s)
    per_seq = s // tm
    n_sel = s // SLC_BLOCK
    sizes = [NSA_WIDTH, 6 * KV_WIDTH, 3 * NSA_HEADS, 2 * CONV_WIDTH, MEM_WIDTH, N_BRANCH * D_MODEL]
    pts = np.cumsum(sizes)[:-1]
    wq, wkv, wg, wc, wm, wgm = [w.astype(BF16) for w in jnp.split(w_in, pts, axis=1)]
    per_group = 3 * NSA_HPG
    wg = jnp.pad(wg.reshape(D_MODEL, NSA_GROUPS, per_group), ((0, 0), (0, 0), (0, LANES - per_group)))
    wg = wg.reshape(D_MODEL, NSA_GROUPS * LANES)
    gq = (jnp.tile(nsa_qk_norm[0], NSA_HEADS) * (HEAD_DIM ** -0.5 * LOG2E))[None, :]
    gks = jnp.tile(nsa_qk_norm[2], NSA_GROUPS)[None, :]
    gkw = jnp.tile(nsa_qk_norm[3], NSA_GROUPS)[None, :]
    gqm = (jnp.tile(mem_qk_norm[0], MEM_HEADS) * (MEM_HEAD_DIM ** -0.5))[None, :]
    bdq = _block_diag(NSA_WIDTH, HEAD_DIM)
    bdk = _block_diag(KV_WIDTH, HEAD_DIM)
    bdm = _block_diag(MEM_WIDTH, MEM_HEAD_DIM)
    consts = [norm_mix[None, :], wq, wkv, wg, wc, wm, wgm, bdq, bdk, bdm, gq, gks, gkw, gqm,
              conv_w.reshape(CONV_KSIZE, CONV_WIDTH), conv_b[None, :], conv_ln_g[None, :], conv_ln_b[None, :]]

    def row(n):
        return pl.BlockSpec((tm, n), lambda i: (i, 0))

    def heads(n_heads, width):
        return pl.BlockSpec((1, n_heads, tm, width), lambda i: (i // per_seq, 0, i % per_seq, 0))

    grp = NSA_GROUPS
    outs = [((b, NSA_HEADS, s, HEAD_DIM), BF16, heads(NSA_HEADS, HEAD_DIM)),
            ((b, grp, s, HEAD_DIM), F32, heads(grp, HEAD_DIM)),
            ((b, grp, s, HEAD_DIM), F32, heads(grp, HEAD_DIM)),
            ((b, grp, s, LANES + n_sel), BF16, heads(grp, LANES + n_sel)),
            ((b, grp, s, LANES), BF16, heads(grp, LANES)),
            ((b, grp, s, LANES), BF16, heads(grp, LANES)),
            ((b, grp, s, LANES), BF16, heads(grp, LANES)),
            ((b, grp, s, LANES), BF16, heads(grp, LANES)),
            ((m, CONV_WIDTH), BF16, row(CONV_WIDTH)),
            ((m, MEM_WIDTH), BF16, row(MEM_WIDTH)),
            ((m, N_BRANCH * D_MODEL), BF16, row(N_BRANCH * D_MODEL))]
    return pl.pallas_call(
        functools.partial(_inproj_kernel, tiles_per_seq=per_seq),
        grid=(m // tm,),
        in_specs=[row(D_MODEL)] + [_const_spec(c.shape) for c in consts],
        out_specs=[spec for _, _, spec in outs],
        out_shape=[jax.ShapeDtypeStruct(shape, dt) for shape, dt, _ in outs],
        scratch_shapes=[pltpu.VMEM((CONV_HALO + tm, CONV_WIDTH), F32),
                        pltpu.VMEM((F32_SUBLANES - 1, tm + CONV_HALO - F32_SUBLANES, CONV_WIDTH), F32)],
        compiler_params=_params("arbitrary"),
        name="inproj",
    )(x.reshape(m, D_MODEL), *consts)


def _gelu_tanh(x):
    return 0.5 * x * (1.0 + jnp.tanh(np.sqrt(2.0 / np.pi) * (x + 0.044715 * (x * x * x))))


def _compress_kernel(x_ref, plo_ref, phi_ref, w1a_ref, w1b_ref, w2_ref, gain_ref, o_ref, *, normalize):
    x = x_ref[0, 0]
    a = _dot((x + plo_ref[...]).astype(BF16), w1a_ref[...])
    b = _dot((x + phi_ref[...]).astype(BF16), w1b_ref[...])
    n = x.shape[0]
    h = a + pltpu.roll(b, n - 1, 0)
    y = _dot(_gelu_tanh(h).astype(BF16), w2_ref[...])
    if normalize:
        ms = jnp.mean(y * y, axis=-1, keepdims=True)
        y = y * lax.rsqrt(ms + RMS_EPS) * gain_ref[...]
    o_ref[0, 0] = y.astype(o_ref.dtype)


def _compress(xc, pos, w1, w2, gain, normalize):
    b, g, n, width = xc.shape
    half = CMP_STRIDE * HEAD_DIM
    plo = pos[:CMP_STRIDE].reshape(1, half)
    phi = pos[CMP_STRIDE:].reshape(1, half)
    w1a = w1[:half].astype(BF16)
    w1b = w1[half:].astype(BF16)
    consts = [plo, phi, w1a, w1b, w2.astype(BF16), gain[None, :]]
    return pl.pallas_call(
        functools.partial(_compress_kernel, normalize=normalize),
        grid=(b, g),
        in_specs=[pl.BlockSpec((1, 1, n, width), lambda i, j: (i, j, 0, 0))] + [_const_spec(c.shape) for c in consts],
        out_specs=pl.BlockSpec((1, 1, n, HEAD_DIM), lambda i, j: (i, j, 0, 0)),
        out_shape=jax.ShapeDtypeStruct((b, g, n, HEAD_DIM), BF16),
        compiler_params=_params("parallel", "parallel"),
        name="compress_k" if normalize else "compress_v",
    )(xc, *consts)


def _nsa_kernel(q_ref, kc_ref, vc_ref, ks_ref, vs_ref, kw_ref, vw_ref, gate_ref, mt_ref, ex_ref, o_ref,
                qaug_scr, m_scr, acc_scr, *, n_cmp, n_sel):
    tq = q_ref.shape[2]
    hpc = NSA_HPG // NSA_CHAINS
    crow = hpc * tq
    chains = range(NSA_CHAINS)
    i = pl.program_id(2)
    q0 = i * tq

    def q_chain(r):
        return q_ref[0, r * hpc:(r + 1) * hpc].reshape(crow, HEAD_DIM)

    def rows_of(r):
        return slice(r * crow, (r + 1) * crow)

    def stack(a, n=hpc):
        return jnp.concatenate([a] * n, axis=0)

    def causal_bias(n, keep):
        r_i = lax.broadcasted_iota(jnp.int32, (n, n), 0)
        c_i = lax.broadcasted_iota(jnp.int32, (n, n), 1)
        return jnp.where(keep(r_i, c_i), 0.0, MASKED)

    t_col = q0 + lax.broadcasted_iota(jnp.int32, (tq, 1), 0)

    rest = LANES - HEAD_DIM
    flag_col = jnp.where(lax.broadcasted_iota(jnp.int32, (crow, rest), 1) == 0, SEL_BIAS, 0.0).astype(BF16)
    for r in chains:
        qaug_scr[rows_of(r), 0:HEAD_DIM] = q_chain(r)
        qaug_scr[rows_of(r), HEAD_DIM:LANES] = flag_col

    cmp_end = lax.broadcasted_iota(jnp.int32, (1, n_cmp), 1) * CMP_STRIDE + (CMP_BLOCK - 1)
    cbias = stack(jnp.where(cmp_end <= t_col, 0.0, MASKED))
    seen = stack(jnp.where(t_col >= CMP_BLOCK - 1, 1.0, 0.0))
    o_c = []
    psum = None
    for r in chains:
        s = _dot_nt(q_chain(r), kc_ref[0, 0]) + cbias
        e = jnp.exp2(s - jnp.max(s, axis=-1, keepdims=True))
        p = e * (seen / jnp.sum(e, axis=-1, keepdims=True))
        o_c.append(_dot(p.astype(BF16), vc_ref[0, 0]))
        for h in range(hpc):
            ph = p[h * tq:(h + 1) * tq]
            psum = ph if psum is None else psum + ph

    hi = psum.astype(BF16)
    lo = (psum - hi.astype(F32)).astype(BF16)
    imp = _dot_nt(mt_ref[...], hi) + _dot_nt(mt_ref[...], lo)
    blk = lax.broadcasted_iota(jnp.int32, (n_sel, tq), 0)
    t_row = q0 + lax.broadcasted_iota(jnp.int32, (n_sel, tq), 1)
    cur = t_row // SLC_BLOCK
    forced = (blk == 0) | (blk == cur) | (blk == cur - 1)
    future = blk * SLC_BLOCK > t_row
    n_forced = 3
    v = jnp.where(forced, -jnp.inf, jnp.where(future, -FORCE, imp))
    blk_f = blk.astype(F32)

    def extract(v, n):
        for _ in range(n):
            top = jnp.max(v, axis=0, keepdims=True)
            first = jnp.min(jnp.where(v == top, blk_f, float(n_sel)), axis=0, keepdims=True)
            v = jnp.where(blk_f == first, -jnp.inf, v)
        return v

    span = WINDOW + WSUB
    pair = NSA_HPG
    pieces = [(sub, hp) for sub in range(tq // WSUB) for hp in range(NSA_HPG // pair)]
    starts = {sub: pl.multiple_of(q0 + sub * WSUB, WSUB) for sub in range(tq // WSUB)}
    win_scores = {}
    for sub, hp in pieces:
        qw = jnp.concatenate([qaug_scr[h * tq + sub * WSUB:h * tq + (sub + 1) * WSUB, 0:LANES]
                              for h in range(hp * pair, (hp + 1) * pair)], axis=0)
        win_scores[sub, hp] = _dot_nt(qw, kw_ref[0, 0, pl.ds(starts[sub], span), :])
    enter = stack(causal_bias(WSUB, lambda r_i, c_i: c_i > r_i), pair)
    leave = stack(causal_bias(WSUB, lambda r_i, c_i: c_i <= r_i), pair)
    todo = min(SLC_TOPK, n_sel) - n_forced
    per_piece = -(-todo // len(pieces))
    o_w = {}
    for sub, hp in pieces:
        sw = win_scores[sub, hp]
        sw = jnp.concatenate([sw[:, :WSUB] + enter, sw[:, WSUB:WINDOW], sw[:, WINDOW:] + leave], axis=1)
        pw = jnp.exp2(sw - jnp.max(sw, axis=-1, keepdims=True))
        o_w[sub, hp] = _dot(pw.astype(BF16), vw_ref[0, 0, pl.ds(starts[sub], span), :])
        v = extract(v, min(per_piece, todo))
        todo -= min(per_piece, todo)
    bias_t = jnp.where(future, SEL_BIAS, jnp.where(v == -jnp.inf, 0.0, SEL_BIAS))
    bias = stack(jnp.transpose(bias_t).astype(BF16))

    for r in chains:
        qaug_scr[rows_of(r), LANES:LANES + n_sel] = bias
    m_scr[...] = jnp.full(m_scr.shape, MASKED, F32)
    acc_scr[...] = jnp.zeros(acc_scr.shape, F32)

    def sel_chunk(k0, tail_bias):
        for r in chains:
            sc = _dot_nt(qaug_scr[rows_of(r)], ks_ref[0, 0, pl.ds(k0, TK), :])
            if tail_bias is not None:
                sc = jnp.concatenate([sc[:, :TK - tq], sc[:, TK - tq:] + tail_bias], axis=1)
            m_old = m_scr[rows_of(r)]
            m_new = jnp.maximum(m_old, jnp.max(sc, axis=-1, keepdims=True))
            pc = jnp.exp2(sc - jnp.concatenate([m_new] * (TK // LANES), axis=1))
            acc_scr[rows_of(r)] = (jnp.exp2(m_old - m_new) * acc_scr[rows_of(r)]
                                   + _dot(pc.astype(BF16), vs_ref[0, 0, pl.ds(k0, TK), :]))
            m_scr[rows_of(r)] = m_new

    sel_chunk(pl.multiple_of(q0, tq), stack(causal_bias(tq, lambda r_i, c_i: c_i <= r_i)))
    n_low = q0 // TK
    low0 = q0 - n_low * TK

    def body(c, carry):
        sel_chunk(pl.multiple_of(low0 + c * TK, tq), None)
        return carry

    lax.fori_loop(0, n_low, body, 0)

    lane = lax.broadcasted_iota(jnp.int32, (tq, LANES), 1)
    gates = gate_ref[0, 0].astype(F32)
    for r in chains:
        acc = acc_scr[rows_of(r)]
        for hh in range(hpc):
            h = r * hpc + hh
            rs = slice(hh * tq, (hh + 1) * tq)
            a_s = acc[rs]
            a_w = jnp.concatenate([o_w[sub, h // pair][(h % pair) * WSUB:(h % pair + 1) * WSUB]
                                   for sub in range(tq // WSUB)], axis=0)
            scal = jnp.where(lane == HEAD_DIM, 1.0 / a_s, jnp.where(lane == W_ONE, 1.0 / a_w, gates))
            y = _dot(scal.astype(BF16), ex_ref[h])

            def slot(k):
                return y[:, k * LANES:k * LANES + HEAD_DIM]

            out = (slot(0) * o_c[r][rs] + (slot(1) * slot(3)) * a_s[:, :HEAD_DIM]
                   + (slot(2) * slot(4)) * a_w[:, :HEAD_DIM])
            o_ref[0, :, h * HEAD_DIM:(h + 1) * HEAD_DIM] = out.astype(o_ref.dtype)


def _importance_matrix(n_cmp, n_sel):
    ratio = SLC_BLOCK // CMP_STRIDE
    c = np.arange(n_cmp)[None, :]
    j = np.arange(n_sel)[:, None]
    return jnp.asarray(((c >= ratio * j - 1) & (c <= ratio * j + ratio - 1)).astype(np.float32), dtype=BF16)


def _spread_matrix():
    e = np.zeros((NSA_HPG, LANES, 5 * LANES), np.float32)
    for h in range(NSA_HPG):
        for k, src in enumerate([3 * h, 3 * h + 1, 3 * h + 2, HEAD_DIM, W_ONE]):
            e[h, src, k * LANES:k * LANES + HEAD_DIM] = 1.0
    return jnp.asarray(e, dtype=BF16)


def _nsa(q, kc, vc, k_aug, vs_aug, kw_aug, vw_aug, gate):
    b, _, s, _ = q.shape
    tq = min(TQ, s)
    n_cmp = s // CMP_STRIDE
    n_sel = s // SLC_BLOCK
    assert n_sel <= LANES and s % TK == 0 and TK % tq == 0 and tq % WSUB == 0 and SEL_PAD == TK - tq

    def left_pad(t, rows, flagged):
        pad = jnp.zeros((rows, t.shape[-1]), BF16)
        if flagged:
            pad = pad.at[:, HEAD_DIM].set(1.0)
        return jnp.concatenate([jnp.broadcast_to(pad, t.shape[:2] + pad.shape), t], axis=2)

    ks_p = left_pad(k_aug, SEL_PAD, True)
    vs_p = left_pad(vs_aug, SEL_PAD, False)
    kw_p = left_pad(kw_aug, WINDOW, True)
    vw_p = left_pad(vw_aug, WINDOW, False)
    mt = _importance_matrix(n_cmp, n_sel)
    ex = _spread_matrix()
    rows = NSA_HPG * tq

    def per_group(shape):
        return pl.BlockSpec((1, 1) + shape, lambda bi, gi, i: (bi, gi, 0, 0))

    return pl.pallas_call(
        functools.partial(_nsa_kernel, n_cmp=n_cmp, n_sel=n_sel),
        grid=(b, NSA_GROUPS, s // tq),
        in_specs=[
            pl.BlockSpec((1, NSA_HPG, tq, HEAD_DIM), lambda bi, gi, i: (bi, gi, i, 0)),
            per_group((n_cmp, HEAD_DIM)), per_group((n_cmp, HEAD_DIM)),
            per_group((s + SEL_PAD, LANES + n_sel)), per_group((s + SEL_PAD, LANES)),
            per_group((s + WINDOW, LANES)), per_group((s + WINDOW, LANES)),
            pl.BlockSpec((1, 1, tq, LANES), lambda bi, gi, i: (bi, gi, i, 0)),
            _const_spec(mt.shape), _const_spec(ex.shape),
        ],
        out_specs=pl.BlockSpec((1, tq, NSA_HPG * HEAD_DIM), lambda bi, gi, i: (bi, i, gi)),
        out_shape=jax.ShapeDtypeStruct((b, s, NSA_WIDTH), BF16),
        scratch_shapes=[pltpu.VMEM((rows, LANES + n_sel), BF16),
                        pltpu.VMEM((rows, LANES), F32),
                        pltpu.VMEM((rows, LANES), F32)],
        compiler_params=_params("parallel", "parallel", "arbitrary"),
        name="nsa",
    )(q, kc, vc, ks_p, vs_p, kw_p, vw_p, gate, mt, ex)


def _memkv_kernel(mem_ref, g_ref, w_ref, bd_ref, gk_ref, km_ref, vm_ref):
    x = mem_ref[0]
    ms = jnp.mean(x * x, axis=-1, keepdims=True)
    xn = (x * lax.rsqrt(ms + RMS_EPS) * g_ref[...]).astype(BF16)
    kv = _dot(xn, w_ref[...])
    km_ref[0] = _head_norm(kv[:, :MEM_WIDTH], bd_ref[...], gk_ref[...]).astype(BF16)
    vm_ref[0] = kv[:, MEM_WIDTH:].astype(BF16)


def _memkv(mem, norm_mem, w_mem_kv, gain_k):
    b, n, _ = mem.shape
    consts = [norm_mem[None, :], w_mem_kv.astype(BF16), _block_diag(MEM_WIDTH, MEM_HEAD_DIM),
              jnp.tile(gain_k, MEM_HEADS)[None, :]]
    spec = pl.BlockSpec((1, n, MEM_WIDTH), lambda i: (i, 0, 0))
    return pl.pallas_call(
        _memkv_kernel,
        grid=(b,),
        in_specs=[pl.BlockSpec((1, n, D_MODEL), lambda i: (i, 0, 0))] + [_const_spec(c.shape) for c in consts],
        out_specs=[spec, spec],
        out_shape=[jax.ShapeDtypeStruct((b, n, MEM_WIDTH), BF16)] * 2,
        compiler_params=_params("parallel"),
        name="memkv",
    )(mem, *consts)


def _merge_kernel(x_ref, onsa_ref, uconv_ref, qm_ref, gm_ref, km_ref, vm_ref,
                  wn_ref, wc_ref, wmo_ref, wo_ref, h_ref):
    qm = qm_ref[...]
    km = km_ref[0]
    vm = vm_ref[0]
    heads = []
    for h in range(MEM_HEADS):
        c = slice(h * MEM_HEAD_DIM, (h + 1) * MEM_HEAD_DIM)
        s = _dot_nt(qm[:, c], km[:, c])
        e = jnp.exp(s - jnp.max(s, axis=-1, keepdims=True))
        p = e * (1.0 / jnp.sum(e, axis=-1, keepdims=True))
        heads.append(_dot(p.astype(BF16), vm[:, c]))
    o_m = jnp.concatenate(heads, axis=1).astype(BF16)
    merged = (gm_ref[:, 0:D_MODEL].astype(F32) * _dot(onsa_ref[...], wn_ref[...])
              + gm_ref[:, D_MODEL:2 * D_MODEL].astype(F32) * _dot(uconv_ref[...], wc_ref[...])
              + gm_ref[:, 2 * D_MODEL:3 * D_MODEL].astype(F32) * _dot(o_m, wmo_ref[...]))
    h_ref[...] = x_ref[...] + _dot(merged.astype(BF16), wo_ref[...])


def _merge(x2, o_nsa, u_conv, qm, gm, km, vm, w_nsa_out, w_conv_out, w_mem_out, w_out, seq):
    m = x2.shape[0]
    tm = min(TM_MERGE, seq)
    per_b = seq // tm
    n_mem = km.shape[1]
    consts = [w_nsa_out.astype(BF16), w_conv_out.astype(BF16), w_mem_out.astype(BF16), w_out.astype(BF16)]

    def row(n):
        return pl.BlockSpec((tm, n), lambda i: (i, 0))

    mem_spec = pl.BlockSpec((1, n_mem, MEM_WIDTH), lambda i: (i // per_b, 0, 0))
    return pl.pallas_call(
        _merge_kernel,
        grid=(m // tm,),
        in_specs=[row(D_MODEL), row(NSA_WIDTH), row(CONV_WIDTH), row(MEM_WIDTH), row(N_BRANCH * D_MODEL),
                  mem_spec, mem_spec] + [_const_spec(c.shape) for c in consts],
        out_specs=row(D_MODEL),
        out_shape=jax.ShapeDtypeStruct((m, D_MODEL), F32),
        compiler_params=_params("parallel"),
        name="merge",
    )(x2, o_nsa, u_conv, qm, gm, km, vm, *consts)


def _ffn_kernel(h_ref, g_ref, wg_ref, wu_ref, wd_ref, o_ref):
    h = h_ref[...]
    ms = jnp.mean(h * h, axis=-1, keepdims=True)
    hn = (h * lax.rsqrt(ms + RMS_EPS) * g_ref[...]).astype(BF16)
    acc = h
    for c in range(0, D_FF, FF_CHUNK):
        gate = _dot(hn, wg_ref[:, c:c + FF_CHUNK])
        up = _dot(hn, wu_ref[:, c:c + FF_CHUNK])
        act = (gate * _sigmoid(gate) * up).astype(BF16)
        acc = acc + _dot(act, wd_ref[c:c + FF_CHUNK, :])
    o_ref[...] = acc


def _ffn(h, norm_ffn, w_gate, w_up, w_down):
    m = h.shape[0]
    tm = min(TM_FFN, m)
    consts = [norm_ffn[None, :], w_gate.astype(BF16), w_up.astype(BF16), w_down.astype(BF16)]
    row = pl.BlockSpec((tm, D_MODEL), lambda i: (i, 0))
    return pl.pallas_call(
        _ffn_kernel,
        grid=(m // tm,),
        in_specs=[row] + [_const_spec(c.shape) for c in consts],
        out_specs=row,
        out_shape=jax.ShapeDtypeStruct((m, D_MODEL), F32),
        compiler_params=_params("parallel"),
        name="ffn",
    )(h, *consts)


def _layer(x, mem, norm_mix, w_in, nsa_qk_norm, cmp_pos, cmp_w1, cmp_w2, w_nsa_out,
           conv_w, conv_b, conv_ln_g, conv_ln_b, w_conv_out, norm_mem, w_mem_kv,
           mem_qk_norm, w_mem_out, w_out, norm_ffn, w_gate, w_up, w_down):
    b, s, _ = x.shape
    x2 = x.reshape(b * s, D_MODEL)
    q, kc_raw, vc_raw, k_aug, vs_aug, kw_aug, vw_aug, gate, u_conv, qm, gm = _inproj(
        x, norm_mix, w_in, nsa_qk_norm, mem_qk_norm, conv_w, conv_b, conv_ln_g, conv_ln_b)

    def chunked(t):
        return t.reshape(b, NSA_GROUPS, s // CMP_STRIDE, CMP_STRIDE * HEAD_DIM)

    kc = _compress(chunked(kc_raw), cmp_pos[0], cmp_w1[0], cmp_w2[0], nsa_qk_norm[1], True)
    vc = _compress(chunked(vc_raw), cmp_pos[1], cmp_w1[1], cmp_w2[1], nsa_qk_norm[1], False)
    o_nsa = _nsa(q, kc, vc, k_aug, vs_aug, kw_aug, vw_aug, gate)

    km, vm = _memkv(mem, norm_mem, w_mem_kv, mem_qk_norm[1])
    h = _merge(x2, o_nsa.reshape(b * s, NSA_WIDTH), u_conv, qm, gm, km, vm,
               w_nsa_out, w_conv_out, w_mem_out, w_out, s)
    out = _ffn(h, norm_ffn, w_gate, w_up, w_down)
    return out.reshape(b, s, D_MODEL)


def kernel(x, mem, norm_mix, w_in, nsa_qk_norm, cmp_pos, cmp_w1, cmp_w2, w_nsa_out, conv_w, conv_b, conv_ln_g, conv_ln_b, w_conv_out, norm_mem, w_mem_kv, mem_qk_norm, w_mem_out, w_out, norm_ffn, w_gate, w_up, w_down):
    h = x
    for l in range(norm_mix.shape[0]):
        h = _layer(h, mem, norm_mix[l], w_in[l], nsa_qk_norm[l], cmp_pos[l], cmp_w1[l], cmp_w2[l],
                   w_nsa_out[l], conv_w[l], conv_b[l], conv_ln_g[l], conv_ln_b[l], w_conv_out[l],
                   norm_mem[l], w_mem_kv[l], mem_qk_norm[l], w_mem_out[l], w_out[l], norm_ffn[l],
                   w_gate[l], w_up[l], w_down[l])
    return h
```

```python
import functools

import numpy as np
import jax
import jax.numpy as jnp
from jax import lax
from jax.experimental import pallas as pl
from jax.experimental.pallas import tpu as pltpu

F32 = jnp.float32
BF16 = jnp.bfloat16

D_MODEL = 1024
HEAD_DIM = 64
NSA_HEADS = 8
NSA_GROUPS = 2
NSA_HPG = NSA_HEADS // NSA_GROUPS
NSA_WIDTH = NSA_HEADS * HEAD_DIM
KV_WIDTH = NSA_GROUPS * HEAD_DIM
CMP_BLOCK = 32
CMP_STRIDE = 16
CMP_HIDDEN = 256
SLC_BLOCK = 64
SLC_TOPK = 16
WINDOW = 512
FORCE = 1e6
CONV_WIDTH = 512
CONV_KSIZE = 31
MEM_HEADS = 4
MEM_HEAD_DIM = 128
MEM_WIDTH = MEM_HEADS * MEM_HEAD_DIM
N_BRANCH = 3
D_FF = 2816
RMS_EPS = 1e-6
LN_EPS = 1e-5

LANES = 128
F32_SUBLANES = 8
VMEM_LIMIT = 56 * 1024 * 1024

TM_PROJ = 512
TQ = 512
TK = 1024
WSUB = 128
NSA_CHAINS = 1
LOG2E = 1.4426950408889634
CONV_HALO = 32
CONV_ROWS = 64
TM_MERGE = 1024
TM_FFN = 1024
FF_CHUNK = D_FF // 11
MASKED = -1e30
SEL_BIAS = -1e9
SEL_PAD = TK - TQ
W_ONE = HEAD_DIM + 1


def _dot(a, b):
    return jnp.dot(a, b, preferred_element_type=F32)


def _dot_nt(a, b):
    return lax.dot_general(a, b, (((1,), (1,)), ((), ())), preferred_element_type=F32)


def _head_norm(t, bd, gain):
    ms = _dot((t * t).astype(BF16), bd)
    return t * lax.rsqrt(ms + RMS_EPS) * gain


def _sigmoid(x):
    return 1.0 / (1.0 + jnp.exp(-x))


def _const_spec(shape):
    nd = len(shape)
    return pl.BlockSpec(shape, lambda *_: (0,) * nd, pipeline_mode=pl.Buffered(1))


def _params(*sem):
    return pltpu.CompilerParams(dimension_semantics=sem, vmem_limit_bytes=VMEM_LIMIT)


def _conv_prepare(buf, shifted):
    for rho in range(1, F32_SUBLANES):
        shifted[rho - 1] = buf[rho:rho + shifted.shape[1]]


def _conv_rows(buf, shifted, w_ref, b_ref, g_ref, beta_ref, o_ref, r0):
    off = CONV_HALO - (CONV_KSIZE - 1)
    acc = jnp.zeros((CONV_ROWS, CONV_WIDTH), F32) + b_ref[...]
    for k in range(CONV_KSIZE):
        rho = (off + k) % F32_SUBLANES
        a = r0 + off + k - rho
        src = buf[a:a + CONV_ROWS] if rho == 0 else shifted[rho - 1, a:a + CONV_ROWS]
        acc = acc + src * w_ref[k:k + 1]
    mu = jnp.mean(acc, axis=-1, keepdims=True)
    d = acc - mu
    var = jnp.mean(d * d, axis=-1, keepdims=True)
    y = d * lax.rsqrt(var + LN_EPS) * g_ref[...] + beta_ref[...]
    o_ref[r0:r0 + CONV_ROWS] = (y * _sigmoid(y)).astype(o_ref.dtype)


def _inproj_kernel(x_ref, g_ref, wq_ref, wkv_ref, wg_ref, wc_ref, wm_ref, wgm_ref,
                   bdq_ref, bdk_ref, bdm_ref, gq_ref, gks_ref, gkw_ref, gqm_ref, cw_ref, cb_ref, lg_ref, lb_ref,
                   q_ref, kc_ref, vc_ref, ks_ref, vs_ref, kw_ref, vw_ref, gn_ref, u_ref, qm_ref, gm_ref,
                   buf, shifted, *, tiles_per_seq):
    tm = x_ref.shape[0]
    n_sel = ks_ref.shape[3] - LANES

    first = pl.program_id(0) % tiles_per_seq == 0

    @pl.when(first)
    def _():
        buf[0:CONV_HALO] = jnp.zeros((CONV_HALO, CONV_WIDTH), F32)

    @pl.when(jnp.logical_not(first))
    def _():
        buf[0:CONV_HALO] = buf[tm:tm + CONV_HALO]

    x = x_ref[...]
    ms = jnp.mean(x * x, axis=-1, keepdims=True)
    xn = (x * lax.rsqrt(ms + RMS_EPS) * g_ref[...]).astype(BF16)

    pc = _dot(xn, wc_ref[...])
    buf[CONV_HALO:] = pc[:, :CONV_WIDTH] * _sigmoid(pc[:, CONV_WIDTH:])
    _conv_prepare(buf, shifted)

    def seg_q():
        qn = _head_norm(_dot(xn, wq_ref[...]), bdq_ref[...], gq_ref[...]).astype(BF16)
        for h in range(NSA_HEADS):
            q_ref[0, h] = qn[:, h * HEAD_DIM:(h + 1) * HEAD_DIM]

    def seg_kv():
        pkv = _dot(xn, wkv_ref[...])
        k_c = pkv[:, 0 * KV_WIDTH:1 * KV_WIDTH]
        v_c = pkv[:, 1 * KV_WIDTH:2 * KV_WIDTH]
        k_s = _head_norm(pkv[:, 2 * KV_WIDTH:3 * KV_WIDTH], bdk_ref[...], gks_ref[...]).astype(BF16)
        v_s = pkv[:, 3 * KV_WIDTH:4 * KV_WIDTH].astype(BF16)
        k_w = _head_norm(pkv[:, 4 * KV_WIDTH:5 * KV_WIDTH], bdk_ref[...], gkw_ref[...]).astype(BF16)
        v_w = pkv[:, 5 * KV_WIDTH:6 * KV_WIDTH].astype(BF16)
        gates = _sigmoid(_dot(xn, wg_ref[...])).astype(BF16)

        t0 = (pl.program_id(0) % tiles_per_seq) * tm
        blk_of_row = (t0 + lax.broadcasted_iota(jnp.int32, (tm, n_sel), 0)) // SLC_BLOCK
        onehot = jnp.where(blk_of_row == lax.broadcasted_iota(jnp.int32, (tm, n_sel), 1), 1.0, 0.0).astype(BF16)
        rest = LANES - HEAD_DIM
        zeros = jnp.zeros((tm, rest), BF16)
        rest_lane = lax.broadcasted_iota(jnp.int32, (tm, rest), 1)
        ones_sel = jnp.where(rest_lane == 0, 1.0, 0.0).astype(BF16)
        ones_win = jnp.where(rest_lane == W_ONE - HEAD_DIM, 1.0, 0.0).astype(BF16)
        for g in range(NSA_GROUPS):
            c = slice(g * HEAD_DIM, (g + 1) * HEAD_DIM)
            kc_ref[0, g] = k_c[:, c]
            vc_ref[0, g] = v_c[:, c]
            ks_ref[0, g, :, 0:HEAD_DIM] = k_s[:, c]
            ks_ref[0, g, :, HEAD_DIM:LANES] = zeros
            ks_ref[0, g, :, LANES:] = onehot
            vs_ref[0, g, :, 0:HEAD_DIM] = v_s[:, c]
            vs_ref[0, g, :, HEAD_DIM:] = ones_sel
            kw_ref[0, g, :, 0:HEAD_DIM] = k_w[:, c]
            kw_ref[0, g, :, HEAD_DIM:] = zeros
            vw_ref[0, g, :, 0:HEAD_DIM] = v_w[:, c]
            vw_ref[0, g, :, HEAD_DIM:] = ones_win
            gn_ref[0, g] = gates[:, g * LANES:(g + 1) * LANES]

    def seg_qm():
        qm_ref[...] = _head_norm(_dot(xn, wm_ref[...]), bdm_ref[...], gqm_ref[...]).astype(BF16)

    def seg_gm(j):
        c = slice(j * D_MODEL, (j + 1) * D_MODEL)
        gm_ref[:, c] = _sigmoid(_dot(xn, wgm_ref[:, c])).astype(BF16)

    segments = [seg_q, seg_kv, seg_qm] + [functools.partial(seg_gm, j) for j in range(N_BRANCH)]
    blocks = list(range(0, tm, CONV_ROWS))
    done = 0
    for j, seg in enumerate(segments):
        seg()
        upto = (j + 1) * len(blocks) // len(segments)
        for r0 in blocks[done:upto]:
            _conv_rows(buf, shifted, cw_ref, cb_ref, lg_ref, lb_ref, u_ref, r0)
        done = upto


def _block_diag(n, blk):
    idx = np.arange(n) // blk
    return jnp.asarray((idx[:, None] == idx[None, :]).astype(np.float32) / blk, dtype=BF16)


def _inproj(x, norm_mix, w_in, nsa_qk_norm, mem_qk_norm, conv_w, conv_b, conv_ln_g, conv_ln_b):
    b, s, _ = x.shape
    m = b * s
    tm = min(TM_PROJ, s)
    per_seq = s // tm
    n_sel = s // SLC_BLOCK
    sizes = [NSA_WIDTH, 6 * KV_WIDTH, 3 * NSA_HEADS, 2 * CONV_WIDTH, MEM_WIDTH, N_BRANCH * D_MODEL]
    pts = np.cumsum(sizes)[:-1]
    wq, wkv, wg, wc, wm, wgm = [w.astype(BF16) for w in jnp.split(w_in, pts, axis=1)]
    per_group = 3 * NSA_HPG
    wg = jnp.pad(wg.reshape(D_MODEL, NSA_GROUPS, per_group), ((0, 0), (0, 0), (0, LANES - per_group)))
    wg = wg.reshape(D_MODEL, NSA_GROUPS * LANES)
    gq = (jnp.tile(nsa_qk_norm[0], NSA_HEADS) * (HEAD_DIM ** -0.5 * LOG2E))[None, :]
    gks = jnp.tile(nsa_qk_norm[2], NSA_GROUPS)[None, :]
    gkw = jnp.tile(nsa_qk_norm[3], NSA_GROUPS)[None, :]
    gqm = (jnp.tile(mem_qk_norm[0], MEM_HEADS) * (MEM_HEAD_DIM ** -0.5))[None, :]
    bdq = _block_diag(NSA_WIDTH, HEAD_DIM)
    bdk = _block_diag(KV_WIDTH, HEAD_DIM)
    bdm = _block_diag(MEM_WIDTH, MEM_HEAD_DIM)
    consts = [norm_mix[None, :], wq, wkv, wg, wc, wm, wgm, bdq, bdk, bdm, gq, gks, gkw, gqm,
              conv_w.reshape(CONV_KSIZE, CONV_WIDTH), conv_b[None, :], conv_ln_g[None, :], conv_ln_b[None, :]]

    def row(n):
        return pl.BlockSpec((tm, n), lambda i: (i, 0))

    def heads(n_heads, width):
        return pl.BlockSpec((1, n_heads, tm, width), lambda i: (i // per_seq, 0, i % per_seq, 0))

    grp = NSA_GROUPS
    outs = [((b, NSA_HEADS, s, HEAD_DIM), BF16, heads(NSA_HEADS, HEAD_DIM)),
            ((b, grp, s, HEAD_DIM), F32, heads(grp, HEAD_DIM)),
            ((b, grp, s, HEAD_DIM), F32, heads(grp, HEAD_DIM)),
            ((b, grp, s, LANES + n_sel), BF16, heads(grp, LANES + n_sel)),
            ((b, grp, s, LANES), BF16, heads(grp, LANES)),
            ((b, grp, s, LANES), BF16, heads(grp, LANES)),
            ((b, grp, s, LANES), BF16, heads(grp, LANES)),
            ((b, grp, s, LANES), BF16, heads(grp, LANES)),
            ((m, CONV_WIDTH), BF16, row(CONV_WIDTH)),
            ((m, MEM_WIDTH), BF16, row(MEM_WIDTH)),
            ((m, N_BRANCH * D_MODEL), BF16, row(N_BRANCH * D_MODEL))]
    return pl.pallas_call(
        functools.partial(_inproj_kernel, tiles_per_seq=per_seq),
        grid=(m // tm,),
        in_specs=[row(D_MODEL)] + [_const_spec(c.shape) for c in consts],
        out_specs=[spec for _, _, spec in outs],
        out_shape=[jax.ShapeDtypeStruct(shape, dt) for shape, dt, _ in outs],
        scratch_shapes=[pltpu.VMEM((CONV_HALO + tm, CONV_WIDTH), F32),
                        pltpu.VMEM((F32_SUBLANES - 1, tm + CONV_HALO - F32_SUBLANES, CONV_WIDTH), F32)],
        compiler_params=_params("arbitrary"),
        name="inproj",
    )(x.reshape(m, D_MODEL), *consts)


def _gelu_tanh(x):
    return 0.5 * x * (1.0 + jnp.tanh(np.sqrt(2.0 / np.pi) * (x + 0.044715 * (x * x * x))))


def _compress_kernel(x_ref, plo_ref, phi_ref, w1a_ref, w1b_ref, w2_ref, gain_ref, o_ref, *, normalize):
    x = x_ref[0, 0]
    a = _dot((x + plo_ref[...]).astype(BF16), w1a_ref[...])
    b = _dot((x + phi_ref[...]).astype(BF16), w1b_ref[...])
    n = x.shape[0]
    h = a + pltpu.roll(b, n - 1, 0)
    y = _dot(_gelu_tanh(h).astype(BF16), w2_ref[...])
    if normalize:
        ms = jnp.mean(y * y, axis=-1, keepdims=True)
        y = y * lax.rsqrt(ms + RMS_EPS) * gain_ref[...]
    o_ref[0, 0] = y.astype(o_ref.dtype)


def _compress(xc, pos, w1, w2, gain, normalize):
    b, g, n, width = xc.shape
    half = CMP_STRIDE * HEAD_DIM
    plo = pos[:CMP_STRIDE].reshape(1, half)
    phi = pos[CMP_STRIDE:].reshape(1, half)
    w1a = w1[:half].astype(BF16)
    w1b = w1[half:].astype(BF16)
    consts = [plo, phi, w1a, w1b, w2.astype(BF16), gain[None, :]]
    return pl.pallas_call(
        functools.partial(_compress_kernel, normalize=normalize),
        grid=(b, g),
        in_specs=[pl.BlockSpec((1, 1, n, width), lambda i, j: (i, j, 0, 0))] + [_const_spec(c.shape) for c in consts],
        out_specs=pl.BlockSpec((1, 1, n, HEAD_DIM), lambda i, j: (i, j, 0, 0)),
        out_shape=jax.ShapeDtypeStruct((b, g, n, HEAD_DIM), BF16),
        compiler_params=_params("parallel", "parallel"),
        name="compress_k" if normalize else "compress_v",
    )(xc, *consts)


def _nsa_kernel(q_ref, kc_ref, vc_ref, ks_ref, vs_ref, kw_ref, vw_ref, gate_ref, mt_ref, ex_ref, o_ref,
                qaug_scr, m_scr, acc_scr, *, n_cmp, n_sel):
    tq = q_ref.shape[2]
    hpc = NSA_HPG // NSA_CHAINS
    crow = hpc * tq
    chains = range(NSA_CHAINS)
    i = pl.program_id(2)
    q0 = i * tq

    def q_chain(r):
        return q_ref[0, r * hpc:(r + 1) * hpc].reshape(crow, HEAD_DIM)

    def rows_of(r):
        return slice(r * crow, (r + 1) * crow)

    def stack(a, n=hpc):
        return jnp.concatenate([a] * n, axis=0)

    def causal_bias(n, keep):
        r_i = lax.broadcasted_iota(jnp.int32, (n, n), 0)
        c_i = lax.broadcasted_iota(jnp.int32, (n, n), 1)
        return jnp.where(keep(r_i, c_i), 0.0, MASKED)

    t_col = q0 + lax.broadcasted_iota(jnp.int32, (tq, 1), 0)

    rest = LANES - HEAD_DIM
    flag_col = jnp.where(lax.broadcasted_iota(jnp.int32, (crow, rest), 1) == 0, SEL_BIAS, 0.0).astype(BF16)
    for r in chains:
        qaug_scr[rows_of(r), 0:HEAD_DIM] = q_chain(r)
        qaug_scr[rows_of(r), HEAD_DIM:LANES] = flag_col

    cmp_end = lax.broadcasted_iota(jnp.int32, (1, n_cmp), 1) * CMP_STRIDE + (CMP_BLOCK - 1)
    cbias = stack(jnp.where(cmp_end <= t_col, 0.0, MASKED))
    seen = stack(jnp.where(t_col >= CMP_BLOCK - 1, 1.0, 0.0))
    o_c = []
    psum = None
    for r in chains:
        s = _dot_nt(q_chain(r), kc_ref[0, 0]) + cbias
        e = jnp.exp2(s - jnp.max(s, axis=-1, keepdims=True))
        p = e * (seen / jnp.sum(e, axis=-1, keepdims=True))
        o_c.append(_dot(p.astype(BF16), vc_ref[0, 0]))
        for h in range(hpc):
            ph = p[h * tq:(h + 1) * tq]
            psum = ph if psum is None else psum + ph

    hi = psum.astype(BF16)
    lo = (psum - hi.astype(F32)).astype(BF16)
    imp = _dot_nt(mt_ref[...], hi) + _dot_nt(mt_ref[...], lo)
    blk = lax.broadcasted_iota(jnp.int32, (n_sel, tq), 0)
    t_row = q0 + lax.broadcasted_iota(jnp.int32, (n_sel, tq), 1)
    cur = t_row // SLC_BLOCK
    forced = (blk == 0) | (blk == cur) | (blk == cur - 1)
    future = blk * SLC_BLOCK > t_row
    n_forced = 3
    v = jnp.where(forced, -jnp.inf, jnp.where(future, -FORCE, imp))
    blk_f = blk.astype(F32)

    def extract(v, n):
        for _ in range(n):
            top = jnp.max(v, axis=0, keepdims=True)
            first = jnp.min(jnp.where(v == top, blk_f, float(n_sel)), axis=0, keepdims=True)
            v = jnp.where(blk_f == first, -jnp.inf, v)
        return v

    span = WINDOW + WSUB
    pair = NSA_HPG
    pieces = [(sub, hp) for sub in range(tq // WSUB) for hp in range(NSA_HPG // pair)]
    starts = {sub: pl.multiple_of(q0 + sub * WSUB, WSUB) for sub in range(tq // WSUB)}
    win_scores = {}
    for sub, hp in pieces:
        qw = jnp.concatenate([qaug_scr[h * tq + sub * WSUB:h * tq + (sub + 1) * WSUB, 0:LANES]
                              for h in range(hp * pair, (hp + 1) * pair)], axis=0)
        win_scores[sub, hp] = _dot_nt(qw, kw_ref[0, 0, pl.ds(starts[sub], span), :])
    enter = stack(causal_bias(WSUB, lambda r_i, c_i: c_i > r_i), pair)
    leave = stack(causal_bias(WSUB, lambda r_i, c_i: c_i <= r_i), pair)
    todo = min(SLC_TOPK, n_sel) - n_forced
    per_piece = -(-todo // len(pieces))
    o_w = {}
    for sub, hp in pieces:
        sw = win_scores[sub, hp]
        sw = jnp.concatenate([sw[:, :WSUB] + enter, sw[:, WSUB:WINDOW], sw[:, WINDOW:] + leave], axis=1)
        pw = jnp.exp2(sw - jnp.max(sw, axis=-1, keepdims=True))
        o_w[sub, hp] = _dot(pw.astype(BF16), vw_ref[0, 0, pl.ds(starts[sub], span), :])
        v = extract(v, min(per_piece, todo))
        todo -= min(per_piece, todo)
    bias_t = jnp.where(future, SEL_BIAS, jnp.where(v == -jnp.inf, 0.0, SEL_BIAS))
    bias = stack(jnp.transpose(bias_t).astype(BF16))

    for r in chains:
        qaug_scr[rows_of(r), LANES:LANES + n_sel] = bias
    m_scr[...] = jnp.full(m_scr.shape, MASKED, F32)
    acc_scr[...] = jnp.zeros(acc_scr.shape, F32)

    def sel_chunk(k0, tail_bias):
        for r in chains:
            sc = _dot_nt(qaug_scr[rows_of(r)], ks_ref[0, 0, pl.ds(k0, TK), :])
            if tail_bias is not None:
                sc = jnp.concatenate([sc[:, :TK - tq], sc[:, TK - tq:] + tail_bias], axis=1)
            m_old = m_scr[rows_of(r)]
            m_new = jnp.maximum(m_old, jnp.max(sc, axis=-1, keepdims=True))
            pc = jnp.exp2(sc - jnp.concatenate([m_new] * (TK // LANES), axis=1))
            acc_scr[rows_of(r)] = (jnp.exp2(m_old - m_new) * acc_scr[rows_of(r)]
                                   + _dot(pc.astype(BF16), vs_ref[0, 0, pl.ds(k0, TK), :]))
            m_scr[rows_of(r)] = m_new

    sel_chunk(pl.multiple_of(q0, tq), stack(causal_bias(tq, lambda r_i, c_i: c_i <= r_i)))
    n_low = q0 // TK
    low0 = q0 - n_low * TK

    def low_chunk(c):
        sel_chunk(pl.multiple_of(low0 + c * TK, tq), None)

    def body(c2, carry):
        low_chunk(2 * c2)
        low_chunk(2 * c2 + 1)
        return carry

    lax.fori_loop(0, n_low // 2, body, 0)

    @pl.when(n_low % 2 == 1)
    def _():
        low_chunk(n_low - 1)

    lane = lax.broadcasted_iota(jnp.int32, (tq, LANES), 1)
    gates = gate_ref[0, 0].astype(F32)
    for r in chains:
        acc = acc_scr[rows_of(r)]
        for hh in range(hpc):
            h = r * hpc + hh
            rs = slice(hh * tq, (hh + 1) * tq)
            a_s = acc[rs]
            a_w = jnp.concatenate([o_w[sub, h // pair][(h % pair) * WSUB:(h % pair + 1) * WSUB]
                                   for sub in range(tq // WSUB)], axis=0)
            scal = jnp.where(lane == HEAD_DIM, 1.0 / a_s, jnp.where(lane == W_ONE, 1.0 / a_w, gates))
            y = _dot(scal.astype(BF16), ex_ref[h])

            def slot(k):
                return y[:, k * LANES:k * LANES + HEAD_DIM]

            out = (slot(0) * o_c[r][rs] + (slot(1) * slot(3)) * a_s[:, :HEAD_DIM]
                   + (slot(2) * slot(4)) * a_w[:, :HEAD_DIM])
            o_ref[0, :, h * HEAD_DIM:(h + 1) * HEAD_DIM] = out.astype(o_ref.dtype)


def _importance_matrix(n_cmp, n_sel):
    ratio = SLC_BLOCK // CMP_STRIDE
    c = np.arange(n_cmp)[None, :]
    j = np.arange(n_sel)[:, None]
    return jnp.asarray(((c >= ratio * j - 1) & (c <= ratio * j + ratio - 1)).astype(np.float32), dtype=BF16)


def _spread_matrix():
    e = np.zeros((NSA_HPG, LANES, 5 * LANES), np.float32)
    for h in range(NSA_HPG):
        for k, src in enumerate([3 * h, 3 * h + 1, 3 * h + 2, HEAD_DIM, W_ONE]):
            e[h, src, k * LANES:k * LANES + HEAD_DIM] = 1.0
    return jnp.asarray(e, dtype=BF16)


def _nsa(q, kc, vc, k_aug, vs_aug, kw_aug, vw_aug, gate):
    b, _, s, _ = q.shape
    tq = min(TQ, s)
    n_cmp = s // CMP_STRIDE
    n_sel = s // SLC_BLOCK
    assert n_sel <= LANES and s % TK == 0 and TK % tq == 0 and tq % WSUB == 0 and SEL_PAD == TK - tq

    def left_pad(t, rows, flagged):
        pad = jnp.zeros((rows, t.shape[-1]), BF16)
        if flagged:
            pad = pad.at[:, HEAD_DIM].set(1.0)
        return jnp.concatenate([jnp.broadcast_to(pad, t.shape[:2] + pad.shape), t], axis=2)

    ks_p = left_pad(k_aug, SEL_PAD, True)
    vs_p = left_pad(vs_aug, SEL_PAD, False)
    kw_p = left_pad(kw_aug, WINDOW, True)
    vw_p = left_pad(vw_aug, WINDOW, False)
    mt = _importance_matrix(n_cmp, n_sel)
    ex = _spread_matrix()
    rows = NSA_HPG * tq

    def per_group(shape):
        return pl.BlockSpec((1, 1) + shape, lambda bi, gi, i: (bi, gi, 0, 0))

    return pl.pallas_call(
        functools.partial(_nsa_kernel, n_cmp=n_cmp, n_sel=n_sel),
        grid=(b, NSA_GROUPS, s // tq),
        in_specs=[
            pl.BlockSpec((1, NSA_HPG, tq, HEAD_DIM), lambda bi, gi, i: (bi, gi, i, 0)),
            per_group((n_cmp, HEAD_DIM)), per_group((n_cmp, HEAD_DIM)),
            per_group((s + SEL_PAD, LANES + n_sel)), per_group((s + SEL_PAD, LANES)),
            per_group((s + WINDOW, LANES)), per_group((s + WINDOW, LANES)),
            pl.BlockSpec((1, 1, tq, LANES), lambda bi, gi, i: (bi, gi, i, 0)),
            _const_spec(mt.shape), _const_spec(ex.shape),
        ],
        out_specs=pl.BlockSpec((1, tq, NSA_HPG * HEAD_DIM), lambda bi, gi, i: (bi, i, gi)),
        out_shape=jax.ShapeDtypeStruct((b, s, NSA_WIDTH), BF16),
        scratch_shapes=[pltpu.VMEM((rows, LANES + n_sel), BF16),
                        pltpu.VMEM((rows, LANES), F32),
                        pltpu.VMEM((rows, LANES), F32)],
        compiler_params=_params("parallel", "parallel", "arbitrary"),
        name="nsa",
    )(q, kc, vc, ks_p, vs_p, kw_p, vw_p, gate, mt, ex)


def _memkv_kernel(mem_ref, g_ref, w_ref, bd_ref, gk_ref, km_ref, vm_ref):
    x = mem_ref[0]
    ms = jnp.mean(x * x, axis=-1, keepdims=True)
    xn = (x * lax.rsqrt(ms + RMS_EPS) * g_ref[...]).astype(BF16)
    kv = _dot(xn, w_ref[...])
    km_ref[0] = _head_norm(kv[:, :MEM_WIDTH], bd_ref[...], gk_ref[...]).astype(BF16)
    vm_ref[0] = kv[:, MEM_WIDTH:].astype(BF16)


def _memkv(mem, norm_mem, w_mem_kv, gain_k):
    b, n, _ = mem.shape
    consts = [norm_mem[None, :], w_mem_kv.astype(BF16), _block_diag(MEM_WIDTH, MEM_HEAD_DIM),
              jnp.tile(gain_k, MEM_HEADS)[None, :]]
    spec = pl.BlockSpec((1, n, MEM_WIDTH), lambda i: (i, 0, 0))
    return pl.pallas_call(
        _memkv_kernel,
        grid=(b,),
        in_specs=[pl.BlockSpec((1, n, D_MODEL), lambda i: (i, 0, 0))] + [_const_spec(c.shape) for c in consts],
        out_specs=[spec, spec],
        out_shape=[jax.ShapeDtypeStruct((b, n, MEM_WIDTH), BF16)] * 2,
        compiler_params=_params("parallel"),
        name="memkv",
    )(mem, *consts)


def _merge_kernel(x_ref, onsa_ref, uconv_ref, qm_ref, gm_ref, km_ref, vm_ref,
                  wn_ref, wc_ref, wmo_ref, wo_ref, h_ref):
    qm = qm_ref[...]
    km = km_ref[0]
    vm = vm_ref[0]
    heads = []
    for h in range(MEM_HEADS):
        c = slice(h * MEM_HEAD_DIM, (h + 1) * MEM_HEAD_DIM)
        s = _dot_nt(qm[:, c], km[:, c])
        e = jnp.exp(s - jnp.max(s, axis=-1, keepdims=True))
        p = e * (1.0 / jnp.sum(e, axis=-1, keepdims=True))
        heads.append(_dot(p.astype(BF16), vm[:, c]))
    o_m = jnp.concatenate(heads, axis=1).astype(BF16)
    merged = (gm_ref[:, 0:D_MODEL].astype(F32) * _dot(onsa_ref[...], wn_ref[...])
              + gm_ref[:, D_MODEL:2 * D_MODEL].astype(F32) * _dot(uconv_ref[...], wc_ref[...])
              + gm_ref[:, 2 * D_MODEL:3 * D_MODEL].astype(F32) * _dot(o_m, wmo_ref[...]))
    h_ref[...] = x_ref[...] + _dot(merged.astype(BF16), wo_ref[...])


def _merge(x2, o_nsa, u_conv, qm, gm, km, vm, w_nsa_out, w_conv_out, w_mem_out, w_out, seq):
    m = x2.shape[0]
    tm = min(TM_MERGE, seq)
    per_b = seq // tm
    n_mem = km.shape[1]
    consts = [w_nsa_out.astype(BF16), w_conv_out.astype(BF16), w_mem_out.astype(BF16), w_out.astype(BF16)]

    def row(n):
        return pl.BlockSpec((tm, n), lambda i: (i, 0))

    mem_spec = pl.BlockSpec((1, n_mem, MEM_WIDTH), lambda i: (i // per_b, 0, 0))
    return pl.pallas_call(
        _merge_kernel,
        grid=(m // tm,),
        in_specs=[row(D_MODEL), row(NSA_WIDTH), row(CONV_WIDTH), row(MEM_WIDTH), row(N_BRANCH * D_MODEL),
                  mem_spec, mem_spec] + [_const_spec(c.shape) for c in consts],
        out_specs=row(D_MODEL),
        out_shape=jax.ShapeDtypeStruct((m, D_MODEL), F32),
        compiler_params=_params("parallel"),
        name="merge",
    )(x2, o_nsa, u_conv, qm, gm, km, vm, *consts)


def _ffn_kernel(h_ref, g_ref, wg_ref, wu_ref, wd_ref, o_ref):
    h = h_ref[...]
    ms = jnp.mean(h * h, axis=-1, keepdims=True)
    hn = (h * lax.rsqrt(ms + RMS_EPS) * g_ref[...]).astype(BF16)
    acc = h
    for c in range(0, D_FF, FF_CHUNK):
        gate = _dot(hn, wg_ref[:, c:c + FF_CHUNK])
        up = _dot(hn, wu_ref[:, c:c + FF_CHUNK])
        act = (gate * _sigmoid(gate) * up).astype(BF16)
        acc = acc + _dot(act, wd_ref[c:c + FF_CHUNK, :])
    o_ref[...] = acc


def _ffn(h, norm_ffn, w_gate, w_up, w_down):
    m = h.shape[0]
    tm = min(TM_FFN, m)
    consts = [norm_ffn[None, :], w_gate.astype(BF16), w_up.astype(BF16), w_down.astype(BF16)]
    row = pl.BlockSpec((tm, D_MODEL), lambda i: (i, 0))
    return pl.pallas_call(
        _ffn_kernel,
        grid=(m // tm,),
        in_specs=[row] + [_const_spec(c.shape) for c in consts],
        out_specs=row,
        out_shape=jax.ShapeDtypeStruct((m, D_MODEL), F32),
        compiler_params=_params("parallel"),
        name="ffn",
    )(h, *consts)


def _layer(x, mem, norm_mix, w_in, nsa_qk_norm, cmp_pos, cmp_w1, cmp_w2, w_nsa_out,
           conv_w, conv_b, conv_ln_g, conv_ln_b, w_conv_out, norm_mem, w_mem_kv,
           mem_qk_norm, w_mem_out, w_out, norm_ffn, w_gate, w_up, w_down):
    b, s, _ = x.shape
    x2 = x.reshape(b * s, D_MODEL)
    q, kc_raw, vc_raw, k_aug, vs_aug, kw_aug, vw_aug, gate, u_conv, qm, gm = _inproj(
        x, norm_mix, w_in, nsa_qk_norm, mem_qk_norm, conv_w, conv_b, conv_ln_g, conv_ln_b)

    def chunked(t):
        return t.reshape(b, NSA_GROUPS, s // CMP_STRIDE, CMP_STRIDE * HEAD_DIM)

    kc = _compress(chunked(kc_raw), cmp_pos[0], cmp_w1[0], cmp_w2[0], nsa_qk_norm[1], True)
    vc = _compress(chunked(vc_raw), cmp_pos[1], cmp_w1[1], cmp_w2[1], nsa_qk_norm[1], False)
    o_nsa = _nsa(q, kc, vc, k_aug, vs_aug, kw_aug, vw_aug, gate)

    km, vm = _memkv(mem, norm_mem, w_mem_kv, mem_qk_norm[1])
    h = _merge(x2, o_nsa.reshape(b * s, NSA_WIDTH), u_conv, qm, gm, km, vm,
               w_nsa_out, w_conv_out, w_mem_out, w_out, s)
    out = _ffn(h, norm_ffn, w_gate, w_up, w_down)
    return out.reshape(b, s, D_MODEL)


def kernel(x, mem, norm_mix, w_in, nsa_qk_norm, cmp_pos, cmp_w1, cmp_w2, w_nsa_out, conv_w, conv_b, conv_ln_g, conv_ln_b, w_conv_out, norm_mem, w_mem_kv, mem_qk_norm, w_mem_out, w_out, norm_ffn, w_gate, w_up, w_down):
    h = x
    for l in range(norm_mix.shape[0]):
        h = _layer(h, mem, norm_mix[l], w_in[l], nsa_qk_norm[l], cmp_pos[l], cmp_w1[l], cmp_w2[l],
                   w_nsa_out[l], conv_w[l], conv_b[l], conv_ln_g[l], conv_ln_b[l], w_conv_out[l],
                   norm_mem[l], w_mem_kv[l], mem_qk_norm[l], w_mem_out[l], w_out[l], norm_ffn[l],
                   w_gate[l], w_up[l], w_down[l])
    return h
```

```python
import functools

import numpy as np
import jax
import jax.numpy as jnp
from jax import lax
from jax.experimental import pallas as pl
from jax.experimental.pallas import tpu as pltpu

F32 = jnp.float32
BF16 = jnp.bfloat16

D_MODEL = 1024
HEAD_DIM = 64
NSA_HEADS = 8
NSA_GROUPS = 2
NSA_HPG = NSA_HEADS // NSA_GROUPS
NSA_WIDTH = NSA_HEADS * HEAD_DIM
KV_WIDTH = NSA_GROUPS * HEAD_DIM
CMP_BLOCK = 32
CMP_STRIDE = 16
CMP_HIDDEN = 256
SLC_BLOCK = 64
SLC_TOPK = 16
WINDOW = 512
FORCE = 1e6
CONV_WIDTH = 512
CONV_KSIZE = 31
MEM_HEADS = 4
MEM_HEAD_DIM = 128
MEM_WIDTH = MEM_HEADS * MEM_HEAD_DIM
N_BRANCH = 3
D_FF = 2816
RMS_EPS = 1e-6
LN_EPS = 1e-5

LANES = 128
F32_SUBLANES = 8
VMEM_LIMIT = 56 * 1024 * 1024

TM_PROJ = 512
TQ = 512
TK = 1024
WSUB = 128
NSA_CHAINS = 1
LOG2E = 1.4426950408889634
CONV_HALO = 32
CONV_ROWS = 64
TM_MERGE = 1024
TM_FFN = 1024
FF_CHUNK = D_FF // 11
MASKED = -1e30
SEL_BIAS = -1e9
SEL_PAD = TK - TQ
W_ONE = HEAD_DIM + 1


def _dot(a, b):
    return jnp.dot(a, b, preferred_element_type=F32)


def _dot_nt(a, b):
    return lax.dot_general(a, b, (((1,), (1,)), ((), ())), preferred_element_type=F32)


def _head_norm(t, bd, gain):
    ms = _dot((t * t).astype(BF16), bd)
    return t * lax.rsqrt(ms + RMS_EPS) * gain


def _sigmoid(x):
    return 1.0 / (1.0 + jnp.exp(-x))


def _const_spec(shape):
    nd = len(shape)
    return pl.BlockSpec(shape, lambda *_: (0,) * nd, pipeline_mode=pl.Buffered(1))


def _params(*sem):
    return pltpu.CompilerParams(dimension_semantics=sem, vmem_limit_bytes=VMEM_LIMIT)


def _conv_prepare(buf, shifted):
    for rho in range(1, F32_SUBLANES):
        shifted[rho - 1] = buf[rho:rho + shifted.shape[1]]


def _conv_rows(buf, shifted, w_ref, b_ref, g_ref, beta_ref, o_ref, r0):
    off = CONV_HALO - (CONV_KSIZE - 1)
    acc = jnp.zeros((CONV_ROWS, CONV_WIDTH), F32) + b_ref[...]
    for k in range(CONV_KSIZE):
        rho = (off + k) % F32_SUBLANES
        a = r0 + off + k - rho
        src = buf[a:a + CONV_ROWS] if rho == 0 else shifted[rho - 1, a:a + CONV_ROWS]
        acc = acc + src * w_ref[k:k + 1]
    mu = jnp.mean(acc, axis=-1, keepdims=True)
    d = acc - mu
    var = jnp.mean(d * d, axis=-1, keepdims=True)
    y = d * lax.rsqrt(var + LN_EPS) * g_ref[...] + beta_ref[...]
    o_ref[r0:r0 + CONV_ROWS] = (y * _sigmoid(y)).astype(o_ref.dtype)


def _inproj_kernel(x_ref, g_ref, wq_ref, wkv_ref, wg_ref, wc_ref, wm_ref, wgm_ref,
                   bdq_ref, bdk_ref, bdm_ref, gq_ref, gks_ref, gkw_ref, gqm_ref, cw_ref, cb_ref, lg_ref, lb_ref,
                   q_ref, kc_ref, vc_ref, ks_ref, vs_ref, kw_ref, vw_ref, gn_ref, u_ref, qm_ref, gm_ref,
                   buf, shifted, *, tiles_per_seq):
    tm = x_ref.shape[0]
    n_sel = ks_ref.shape[3] - LANES

    first = pl.program_id(0) % tiles_per_seq == 0

    @pl.when(first)
    def _():
        buf[0:CONV_HALO] = jnp.zeros((CONV_HALO, CONV_WIDTH), F32)

    @pl.when(jnp.logical_not(first))
    def _():
        buf[0:CONV_HALO] = buf[tm:tm + CONV_HALO]

    x = x_ref[...]
    ms = jnp.mean(x * x, axis=-1, keepdims=True)
    xn = (x * lax.rsqrt(ms + RMS_EPS) * g_ref[...]).astype(BF16)

    pc = _dot(xn, wc_ref[...])
    buf[CONV_HALO:] = pc[:, :CONV_WIDTH] * _sigmoid(pc[:, CONV_WIDTH:])
    _conv_prepare(buf, shifted)

    def seg_q():
        qn = _head_norm(_dot(xn, wq_ref[...]), bdq_ref[...], gq_ref[...]).astype(BF16)
        for h in range(NSA_HEADS):
            q_ref[0, h] = qn[:, h * HEAD_DIM:(h + 1) * HEAD_DIM]

    def seg_kv():
        pkv = _dot(xn, wkv_ref[...])
        k_c = pkv[:, 0 * KV_WIDTH:1 * KV_WIDTH]
        v_c = pkv[:, 1 * KV_WIDTH:2 * KV_WIDTH]
        k_s = _head_norm(pkv[:, 2 * KV_WIDTH:3 * KV_WIDTH], bdk_ref[...], gks_ref[...]).astype(BF16)
        v_s = pkv[:, 3 * KV_WIDTH:4 * KV_WIDTH].astype(BF16)
        k_w = _head_norm(pkv[:, 4 * KV_WIDTH:5 * KV_WIDTH], bdk_ref[...], gkw_ref[...]).astype(BF16)
        v_w = pkv[:, 5 * KV_WIDTH:6 * KV_WIDTH].astype(BF16)
        gates = _sigmoid(_dot(xn, wg_ref[...])).astype(BF16)

        t0 = (pl.program_id(0) % tiles_per_seq) * tm
        blk_of_row = (t0 + lax.broadcasted_iota(jnp.int32, (tm, n_sel), 0)) // SLC_BLOCK
        onehot = jnp.where(blk_of_row == lax.broadcasted_iota(jnp.int32, (tm, n_sel), 1), 1.0, 0.0).astype(BF16)
        rest = LANES - HEAD_DIM
        zeros = jnp.zeros((tm, rest), BF16)
        rest_lane = lax.broadcasted_iota(jnp.int32, (tm, rest), 1)
        ones_sel = jnp.where(rest_lane == 0, 1.0, 0.0).astype(BF16)
        ones_win = jnp.where(rest_lane == W_ONE - HEAD_DIM, 1.0, 0.0).astype(BF16)
        for g in range(NSA_GROUPS):
            c = slice(g * HEAD_DIM, (g + 1) * HEAD_DIM)
            kc_ref[0, g] = k_c[:, c]
            vc_ref[0, g] = v_c[:, c]
            ks_ref[0, g, :, 0:HEAD_DIM] = k_s[:, c]
            ks_ref[0, g, :, HEAD_DIM:LANES] = zeros
            ks_ref[0, g, :, LANES:] = onehot
            vs_ref[0, g, :, 0:HEAD_DIM] = v_s[:, c]
            vs_ref[0, g, :, HEAD_DIM:] = ones_sel
            kw_ref[0, g, :, 0:HEAD_DIM] = k_w[:, c]
            kw_ref[0, g, :, HEAD_DIM:] = zeros
            vw_ref[0, g, :, 0:HEAD_DIM] = v_w[:, c]
            vw_ref[0, g, :, HEAD_DIM:] = ones_win
            gn_ref[0, g] = gates[:, g * LANES:(g + 1) * LANES]

    def seg_qm():
        qm_ref[...] = _head_norm(_dot(xn, wm_ref[...]), bdm_ref[...], gqm_ref[...]).astype(BF16)

    def seg_gm(j):
        c = slice(j * D_MODEL, (j + 1) * D_MODEL)
        gm_ref[:, c] = _sigmoid(_dot(xn, wgm_ref[:, c])).astype(BF16)

    segments = [seg_q, seg_kv, seg_qm] + [functools.partial(seg_gm, j) for j in range(N_BRANCH)]
    blocks = list(range(0, tm, CONV_ROWS))
    done = 0
    for j, seg in enumerate(segments):
        seg()
        upto = (j + 1) * len(blocks) // len(segments)
        for r0 in blocks[done:upto]:
            _conv_rows(buf, shifted, cw_ref, cb_ref, lg_ref, lb_ref, u_ref, r0)
        done = upto


def _block_diag(n, blk):
    idx = np.arange(n) // blk
    return jnp.asarray((idx[:, None] == idx[None, :]).astype(np.float32) / blk, dtype=BF16)


def _inproj(x, norm_mix, w_in, nsa_qk_norm, mem_qk_norm, conv_w, conv_b, conv_ln_g, conv_ln_b):
    b, s, _ = x.shape
    m = b * s
    tm = min(TM_PROJ, s)
    per_seq = s // tm
    n_sel = s // SLC_BLOCK
    sizes = [NSA_WIDTH, 6 * KV_WIDTH, 3 * NSA_HEADS, 2 * CONV_WIDTH, MEM_WIDTH, N_BRANCH * D_MODEL]
    pts = np.cumsum(sizes)[:-1]
    wq, wkv, wg, wc, wm, wgm = [w.astype(BF16) for w in jnp.split(w_in, pts, axis=1)]
    per_group = 3 * NSA_HPG
    wg = jnp.pad(wg.reshape(D_MODEL, NSA_GROUPS, per_group), ((0, 0), (0, 0), (0, LANES - per_group)))
    wg = wg.reshape(D_MODEL, NSA_GROUPS * LANES)
    gq = (jnp.tile(nsa_qk_norm[0], NSA_HEADS) * (HEAD_DIM ** -0.5 * LOG2E))[None, :]
    gks = jnp.tile(nsa_qk_norm[2], NSA_GROUPS)[None, :]
    gkw = jnp.tile(nsa_qk_norm[3], NSA_GROUPS)[None, :]
    gqm = (jnp.tile(mem_qk_norm[0], MEM_HEADS) * (MEM_HEAD_DIM ** -0.5))[None, :]
    bdq = _block_diag(NSA_WIDTH, HEAD_DIM)
    bdk = _block_diag(KV_WIDTH, HEAD_DIM)
    bdm = _block_diag(MEM_WIDTH, MEM_HEAD_DIM)
    consts = [norm_mix[None, :], wq, wkv, wg, wc, wm, wgm, bdq, bdk, bdm, gq, gks, gkw, gqm,
              conv_w.reshape(CONV_KSIZE, CONV_WIDTH), conv_b[None, :], conv_ln_g[None, :], conv_ln_b[None, :]]

    def row(n):
        return pl.BlockSpec((tm, n), lambda i: (i, 0))

    def heads(n_heads, width):
        return pl.BlockSpec((1, n_heads, tm, width), lambda i: (i // per_seq, 0, i % per_seq, 0))

    grp = NSA_GROUPS
    outs = [((b, NSA_HEADS, s, HEAD_DIM), BF16, heads(NSA_HEADS, HEAD_DIM)),
            ((b, grp, s, HEAD_DIM), F32, heads(grp, HEAD_DIM)),
            ((b, grp, s, HEAD_DIM), F32, heads(grp, HEAD_DIM)),
            ((b, grp, s, LANES + n_sel), BF16, heads(grp, LANES + n_sel)),
            ((b, grp, s, LANES), BF16, heads(grp, LANES)),
            ((b, grp, s, LANES), BF16, heads(grp, LANES)),
            ((b, grp, s, LANES), BF16, heads(grp, LANES)),
            ((b, grp, s, LANES), BF16, heads(grp, LANES)),
            ((m, CONV_WIDTH), BF16, row(CONV_WIDTH)),
            ((m, MEM_WIDTH), BF16, row(MEM_WIDTH)),
            ((m, N_BRANCH * D_MODEL), BF16, row(N_BRANCH * D_MODEL))]
    return pl.pallas_call(
        functools.partial(_inproj_kernel, tiles_per_seq=per_seq),
        grid=(m // tm,),
        in_specs=[row(D_MODEL)] + [_const_spec(c.shape) for c in consts],
        out_specs=[spec for _, _, spec in outs],
        out_shape=[jax.ShapeDtypeStruct(shape, dt) for shape, dt, _ in outs],
        scratch_shapes=[pltpu.VMEM((CONV_HALO + tm, CONV_WIDTH), F32),
                        pltpu.VMEM((F32_SUBLANES - 1, tm + CONV_HALO - F32_SUBLANES, CONV_WIDTH), F32)],
        compiler_params=_params("arbitrary"),
        name="inproj",
    )(x.reshape(m, D_MODEL), *consts)


def _gelu_tanh(x):
    return 0.5 * x * (1.0 + jnp.tanh(np.sqrt(2.0 / np.pi) * (x + 0.044715 * (x * x * x))))


def _compress_kernel(x_ref, plo_ref, phi_ref, w1a_ref, w1b_ref, w2_ref, gain_ref, o_ref, *, normalize):
    n = x_ref.shape[2] // CMP_STRIDE
    x = jnp.concatenate([x_ref[0, 0, pl.ds(r, n, stride=CMP_STRIDE), :] for r in range(CMP_STRIDE)], axis=1)
    a = _dot((x + plo_ref[...]).astype(BF16), w1a_ref[...])
    b = _dot((x + phi_ref[...]).astype(BF16), w1b_ref[...])
    h = a + pltpu.roll(b, n - 1, 0)
    y = _dot(_gelu_tanh(h).astype(BF16), w2_ref[...])
    if normalize:
        ms = jnp.mean(y * y, axis=-1, keepdims=True)
        y = y * lax.rsqrt(ms + RMS_EPS) * gain_ref[...]
    o_ref[0, 0] = y.astype(o_ref.dtype)


def _compress(xc, pos, w1, w2, gain, normalize):
    b, g, s, width = xc.shape
    n = s // CMP_STRIDE
    half = CMP_STRIDE * HEAD_DIM
    plo = pos[:CMP_STRIDE].reshape(1, half)
    phi = pos[CMP_STRIDE:].reshape(1, half)
    w1a = w1[:half].astype(BF16)
    w1b = w1[half:].astype(BF16)
    consts = [plo, phi, w1a, w1b, w2.astype(BF16), gain[None, :]]
    return pl.pallas_call(
        functools.partial(_compress_kernel, normalize=normalize),
        grid=(b, g),
        in_specs=[pl.BlockSpec((1, 1, s, width), lambda i, j: (i, j, 0, 0))] + [_const_spec(c.shape) for c in consts],
        out_specs=pl.BlockSpec((1, 1, n, HEAD_DIM), lambda i, j: (i, j, 0, 0)),
        out_shape=jax.ShapeDtypeStruct((b, g, n, HEAD_DIM), BF16),
        compiler_params=_params("parallel", "parallel"),
        name="compress_k" if normalize else "compress_v",
    )(xc, *consts)


def _nsa_kernel(q_ref, kc_ref, vc_ref, ks_ref, vs_ref, kw_ref, vw_ref, gate_ref, mt_ref, ex_ref, o_ref,
                qaug_scr, m_scr, acc_scr, *, n_cmp, n_sel):
    tq = q_ref.shape[2]
    hpc = NSA_HPG // NSA_CHAINS
    crow = hpc * tq
    chains = range(NSA_CHAINS)
    i = pl.program_id(2)
    q0 = i * tq

    def q_chain(r):
        return q_ref[0, r * hpc:(r + 1) * hpc].reshape(crow, HEAD_DIM)

    def rows_of(r):
        return slice(r * crow, (r + 1) * crow)

    def stack(a, n=hpc):
        return jnp.concatenate([a] * n, axis=0)

    def causal_bias(n, keep):
        r_i = lax.broadcasted_iota(jnp.int32, (n, n), 0)
        c_i = lax.broadcasted_iota(jnp.int32, (n, n), 1)
        return jnp.where(keep(r_i, c_i), 0.0, MASKED)

    t_col = q0 + lax.broadcasted_iota(jnp.int32, (tq, 1), 0)

    rest = LANES - HEAD_DIM
    flag_col = jnp.where(lax.broadcasted_iota(jnp.int32, (crow, rest), 1) == 0, SEL_BIAS, 0.0).astype(BF16)
    for r in chains:
        qaug_scr[rows_of(r), 0:HEAD_DIM] = q_chain(r)
        qaug_scr[rows_of(r), HEAD_DIM:LANES] = flag_col

    cmp_end = lax.broadcasted_iota(jnp.int32, (1, n_cmp), 1) * CMP_STRIDE + (CMP_BLOCK - 1)
    cbias = stack(jnp.where(cmp_end <= t_col, 0.0, MASKED))
    seen = stack(jnp.where(t_col >= CMP_BLOCK - 1, 1.0, 0.0))
    o_c = []
    psum = None
    for r in chains:
        s = _dot_nt(q_chain(r), kc_ref[0, 0]) + cbias
        e = jnp.exp2(s - jnp.max(s, axis=-1, keepdims=True))
        p = e * (seen / jnp.sum(e, axis=-1, keepdims=True))
        o_c.append(_dot(p.astype(BF16), vc_ref[0, 0]))
        for h in range(hpc):
            ph = p[h * tq:(h + 1) * tq]
            psum = ph if psum is None else psum + ph

    hi = psum.astype(BF16)
    lo = (psum - hi.astype(F32)).astype(BF16)
    imp = _dot_nt(mt_ref[...], hi) + _dot_nt(mt_ref[...], lo)
    blk = lax.broadcasted_iota(jnp.int32, (n_sel, tq), 0)
    t_row = q0 + lax.broadcasted_iota(jnp.int32, (n_sel, tq), 1)
    cur = t_row // SLC_BLOCK
    forced = (blk == 0) | (blk == cur) | (blk == cur - 1)
    future = blk * SLC_BLOCK > t_row
    n_forced = 3
    v = jnp.where(forced, -jnp.inf, jnp.where(future, -FORCE, imp))
    blk_f = blk.astype(F32)

    def extract(v, n):
        for _ in range(n):
            top = jnp.max(v, axis=0, keepdims=True)
            first = jnp.min(jnp.where(v == top, blk_f, float(n_sel)), axis=0, keepdims=True)
            v = jnp.where(blk_f == first, -jnp.inf, v)
        return v

    span = WINDOW + WSUB
    pair = NSA_HPG
    pieces = [(sub, hp) for sub in range(tq // WSUB) for hp in range(NSA_HPG // pair)]
    starts = {sub: pl.multiple_of(q0 + sub * WSUB, WSUB) for sub in range(tq // WSUB)}
    win_scores = {}
    for sub, hp in pieces:
        qw = jnp.concatenate([qaug_scr[h * tq + sub * WSUB:h * tq + (sub + 1) * WSUB, 0:LANES]
                              for h in range(hp * pair, (hp + 1) * pair)], axis=0)
        win_scores[sub, hp] = _dot_nt(qw, kw_ref[0, 0, pl.ds(starts[sub], span), :])
    enter = stack(causal_bias(WSUB, lambda r_i, c_i: c_i > r_i), pair)
    leave = stack(causal_bias(WSUB, lambda r_i, c_i: c_i <= r_i), pair)
    todo = min(SLC_TOPK, n_sel) - n_forced
    per_piece = -(-todo // len(pieces))
    o_w = {}
    for sub, hp in pieces:
        sw = win_scores[sub, hp]
        sw = jnp.concatenate([sw[:, :WSUB] + enter, sw[:, WSUB:WINDOW], sw[:, WINDOW:] + leave], axis=1)
        pw = jnp.exp2(sw - jnp.max(sw, axis=-1, keepdims=True))
        o_w[sub, hp] = _dot(pw.astype(BF16), vw_ref[0, 0, pl.ds(starts[sub], span), :])
        v = extract(v, min(per_piece, todo))
        todo -= min(per_piece, todo)
    bias_t = jnp.where(future, SEL_BIAS, jnp.where(v == -jnp.inf, 0.0, SEL_BIAS))
    bias = stack(jnp.transpose(bias_t).astype(BF16))

    for r in chains:
        qaug_scr[rows_of(r), LANES:LANES + n_sel] = bias
    m_scr[...] = jnp.full(m_scr.shape, MASKED, F32)
    acc_scr[...] = jnp.zeros(acc_scr.shape, F32)

    def sel_chunk(k0, tail_bias):
        for r in chains:
            sc = _dot_nt(qaug_scr[rows_of(r)], ks_ref[0, 0, pl.ds(k0, TK), :])
            if tail_bias is not None:
                sc = jnp.concatenate([sc[:, :TK - tq], sc[:, TK - tq:] + tail_bias], axis=1)
            m_old = m_scr[rows_of(r)]
            m_new = jnp.maximum(m_old, jnp.max(sc, axis=-1, keepdims=True))
            pc = jnp.exp2(sc - jnp.concatenate([m_new] * (TK // LANES), axis=1))
            acc_scr[rows_of(r)] = (jnp.exp2(m_old - m_new) * acc_scr[rows_of(r)]
                                   + _dot(pc.astype(BF16), vs_ref[0, 0, pl.ds(k0, TK), :]))
            m_scr[rows_of(r)] = m_new

    sel_chunk(pl.multiple_of(q0, tq), stack(causal_bias(tq, lambda r_i, c_i: c_i <= r_i)))
    n_low = q0 // TK
    low0 = q0 - n_low * TK

    def low_chunk(c):
        sel_chunk(pl.multiple_of(low0 + c * TK, tq), None)

    def body(c2, carry):
        low_chunk(2 * c2)
        low_chunk(2 * c2 + 1)
        return carry

    lax.fori_loop(0, n_low // 2, body, 0)

    @pl.when(n_low % 2 == 1)
    def _():
        low_chunk(n_low - 1)

    lane = lax.broadcasted_iota(jnp.int32, (tq, LANES), 1)
    gates = gate_ref[0, 0].astype(F32)
    for r in chains:
        acc = acc_scr[rows_of(r)]
        for hh in range(hpc):
            h = r * hpc + hh
            rs = slice(hh * tq, (hh + 1) * tq)
            a_s = acc[rs]
            a_w = jnp.concatenate([o_w[sub, h // pair][(h % pair) * WSUB:(h % pair + 1) * WSUB]
                                   for sub in range(tq // WSUB)], axis=0)
            scal = jnp.where(lane == HEAD_DIM, 1.0 / a_s, jnp.where(lane == W_ONE, 1.0 / a_w, gates))
            y = _dot(scal.astype(BF16), ex_ref[h])

            def slot(k):
                return y[:, k * LANES:k * LANES + HEAD_DIM]

            out = (slot(0) * o_c[r][rs] + (slot(1) * slot(3)) * a_s[:, :HEAD_DIM]
                   + (slot(2) * slot(4)) * a_w[:, :HEAD_DIM])
            o_ref[0, :, h * HEAD_DIM:(h + 1) * HEAD_DIM] = out.astype(o_ref.dtype)


def _importance_matrix(n_cmp, n_sel):
    ratio = SLC_BLOCK // CMP_STRIDE
    c = np.arange(n_cmp)[None, :]
    j = np.arange(n_sel)[:, None]
    return jnp.asarray(((c >= ratio * j - 1) & (c <= ratio * j + ratio - 1)).astype(np.float32), dtype=BF16)


def _spread_matrix():
    e = np.zeros((NSA_HPG, LANES, 5 * LANES), np.float32)
    for h in range(NSA_HPG):
        for k, src in enumerate([3 * h, 3 * h + 1, 3 * h + 2, HEAD_DIM, W_ONE]):
            e[h, src, k * LANES:k * LANES + HEAD_DIM] = 1.0
    return jnp.asarray(e, dtype=BF16)


def _nsa(q, kc, vc, k_aug, vs_aug, kw_aug, vw_aug, gate):
    b, _, s, _ = q.shape
    tq = min(TQ, s)
    n_cmp = s // CMP_STRIDE
    n_sel = s // SLC_BLOCK
    assert n_sel <= LANES and s % TK == 0 and TK % tq == 0 and tq % WSUB == 0 and SEL_PAD == TK - tq

    def left_pad(t, rows, flagged):
        pad = jnp.zeros((rows, t.shape[-1]), BF16)
        if flagged:
            pad = pad.at[:, HEAD_DIM].set(1.0)
        return jnp.concatenate([jnp.broadcast_to(pad, t.shape[:2] + pad.shape), t], axis=2)

    ks_p = left_pad(k_aug, SEL_PAD, True)
    vs_p = left_pad(vs_aug, SEL_PAD, False)
    kw_p = left_pad(kw_aug, WINDOW, True)
    vw_p = left_pad(vw_aug, WINDOW, False)
    mt = _importance_matrix(n_cmp, n_sel)
    ex = _spread_matrix()
    rows = NSA_HPG * tq

    def per_group(shape):
        return pl.BlockSpec((1, 1) + shape, lambda bi, gi, i: (bi, gi, 0, 0))

    return pl.pallas_call(
        functools.partial(_nsa_kernel, n_cmp=n_cmp, n_sel=n_sel),
        grid=(b, NSA_GROUPS, s // tq),
        in_specs=[
            pl.BlockSpec((1, NSA_HPG, tq, HEAD_DIM), lambda bi, gi, i: (bi, gi, i, 0)),
            per_group((n_cmp, HEAD_DIM)), per_group((n_cmp, HEAD_DIM)),
            per_group((s + SEL_PAD, LANES + n_sel)), per_group((s + SEL_PAD, LANES)),
            per_group((s + WINDOW, LANES)), per_group((s + WINDOW, LANES)),
            pl.BlockSpec((1, 1, tq, LANES), lambda bi, gi, i: (bi, gi, i, 0)),
            _const_spec(mt.shape), _const_spec(ex.shape),
        ],
        out_specs=pl.BlockSpec((1, tq, NSA_HPG * HEAD_DIM), lambda bi, gi, i: (bi, i, gi)),
        out_shape=jax.ShapeDtypeStruct((b, s, NSA_WIDTH), BF16),
        scratch_shapes=[pltpu.VMEM((rows, LANES + n_sel), BF16),
                        pltpu.VMEM((rows, LANES), F32),
                        pltpu.VMEM((rows, LANES), F32)],
        compiler_params=_params("parallel", "parallel", "arbitrary"),
        name="nsa",
    )(q, kc, vc, ks_p, vs_p, kw_p, vw_p, gate, mt, ex)


def _memkv_kernel(mem_ref, g_ref, w_ref, bd_ref, gk_ref, km_ref, vm_ref):
    x = mem_ref[0]
    ms = jnp.mean(x * x, axis=-1, keepdims=True)
    xn = (x * lax.rsqrt(ms + RMS_EPS) * g_ref[...]).astype(BF16)
    kv = _dot(xn, w_ref[...])
    km_ref[0] = _head_norm(kv[:, :MEM_WIDTH], bd_ref[...], gk_ref[...]).astype(BF16)
    vm_ref[0] = kv[:, MEM_WIDTH:].astype(BF16)


def _memkv(mem, norm_mem, w_mem_kv, gain_k):
    b, n, _ = mem.shape
    consts = [norm_mem[None, :], w_mem_kv.astype(BF16), _block_diag(MEM_WIDTH, MEM_HEAD_DIM),
              jnp.tile(gain_k, MEM_HEADS)[None, :]]
    spec = pl.BlockSpec((1, n, MEM_WIDTH), lambda i: (i, 0, 0))
    return pl.pallas_call(
        _memkv_kernel,
        grid=(b,),
        in_specs=[pl.BlockSpec((1, n, D_MODEL), lambda i: (i, 0, 0))] + [_const_spec(c.shape) for c in consts],
        out_specs=[spec, spec],
        out_shape=[jax.ShapeDtypeStruct((b, n, MEM_WIDTH), BF16)] * 2,
        compiler_params=_params("parallel"),
        name="memkv",
    )(mem, *consts)


def _merge_kernel(x_ref, onsa_ref, uconv_ref, qm_ref, gm_ref, km_ref, vm_ref,
                  wn_ref, wc_ref, wmo_ref, wo_ref, h_ref):
    qm = qm_ref[...]
    km = km_ref[0]
    vm = vm_ref[0]
    heads = []
    for h in range(MEM_HEADS):
        c = slice(h * MEM_HEAD_DIM, (h + 1) * MEM_HEAD_DIM)
        s = _dot_nt(qm[:, c], km[:, c])
        e = jnp.exp(s - jnp.max(s, axis=-1, keepdims=True))
        p = e * (1.0 / jnp.sum(e, axis=-1, keepdims=True))
        heads.append(_dot(p.astype(BF16), vm[:, c]))
    o_m = jnp.concatenate(heads, axis=1).astype(BF16)
    merged = (gm_ref[:, 0:D_MODEL].astype(F32) * _dot(onsa_ref[...], wn_ref[...])
              + gm_ref[:, D_MODEL:2 * D_MODEL].astype(F32) * _dot(uconv_ref[...], wc_ref[...])
              + gm_ref[:, 2 * D_MODEL:3 * D_MODEL].astype(F32) * _dot(o_m, wmo_ref[...]))
    h_ref[...] = x_ref[...] + _dot(merged.astype(BF16), wo_ref[...])


def _merge(x2, o_nsa, u_conv, qm, gm, km, vm, w_nsa_out, w_conv_out, w_mem_out, w_out, seq):
    m = x2.shape[0]
    tm = min(TM_MERGE, seq)
    per_b = seq // tm
    n_mem = km.shape[1]
    consts = [w_nsa_out.astype(BF16), w_conv_out.astype(BF16), w_mem_out.astype(BF16), w_out.astype(BF16)]

    def row(n):
        return pl.BlockSpec((tm, n), lambda i: (i, 0))

    mem_spec = pl.BlockSpec((1, n_mem, MEM_WIDTH), lambda i: (i // per_b, 0, 0))
    return pl.pallas_call(
        _merge_kernel,
        grid=(m // tm,),
        in_specs=[row(D_MODEL), row(NSA_WIDTH), row(CONV_WIDTH), row(MEM_WIDTH), row(N_BRANCH * D_MODEL),
                  mem_spec, mem_spec] + [_const_spec(c.shape) for c in consts],
        out_specs=row(D_MODEL),
        out_shape=jax.ShapeDtypeStruct((m, D_MODEL), F32),
        compiler_params=_params("parallel"),
        name="merge",
    )(x2, o_nsa, u_conv, qm, gm, km, vm, *consts)


def _ffn_kernel(h_ref, g_ref, wg_ref, wu_ref, wd_ref, o_ref):
    h = h_ref[...]
    ms = jnp.mean(h * h, axis=-1, keepdims=True)
    hn = (h * lax.rsqrt(ms + RMS_EPS) * g_ref[...]).astype(BF16)
    acc = h
    for c in range(0, D_FF, FF_CHUNK):
        gate = _dot(hn, wg_ref[:, c:c + FF_CHUNK])
        up = _dot(hn, wu_ref[:, c:c + FF_CHUNK])
        act = (gate * _sigmoid(gate) * up).astype(BF16)
        acc = acc + _dot(act, wd_ref[c:c + FF_CHUNK, :])
    o_ref[...] = acc


def _ffn(h, norm_ffn, w_gate, w_up, w_down):
    m = h.shape[0]
    tm = min(TM_FFN, m)
    consts = [norm_ffn[None, :], w_gate.astype(BF16), w_up.astype(BF16), w_down.astype(BF16)]
    row = pl.BlockSpec((tm, D_MODEL), lambda i: (i, 0))
    return pl.pallas_call(
        _ffn_kernel,
        grid=(m // tm,),
        in_specs=[row] + [_const_spec(c.shape) for c in consts],
        out_specs=row,
        out_shape=jax.ShapeDtypeStruct((m, D_MODEL), F32),
        compiler_params=_params("parallel"),
        name="ffn",
    )(h, *consts)


def _layer(x, mem, norm_mix, w_in, nsa_qk_norm, cmp_pos, cmp_w1, cmp_w2, w_nsa_out,
           conv_w, conv_b, conv_ln_g, conv_ln_b, w_conv_out, norm_mem, w_mem_kv,
           mem_qk_norm, w_mem_out, w_out, norm_ffn, w_gate, w_up, w_down):
    b, s, _ = x.shape
    x2 = x.reshape(b * s, D_MODEL)
    q, kc_raw, vc_raw, k_aug, vs_aug, kw_aug, vw_aug, gate, u_conv, qm, gm = _inproj(
        x, norm_mix, w_in, nsa_qk_norm, mem_qk_norm, conv_w, conv_b, conv_ln_g, conv_ln_b)

    kc = _compress(kc_raw, cmp_pos[0], cmp_w1[0], cmp_w2[0], nsa_qk_norm[1], True)
    vc = _compress(vc_raw, cmp_pos[1], cmp_w1[1], cmp_w2[1], nsa_qk_norm[1], False)
    o_nsa = _nsa(q, kc, vc, k_aug, vs_aug, kw_aug, vw_aug, gate)

    km, vm = _memkv(mem, norm_mem, w_mem_kv, mem_qk_norm[1])
    h = _merge(x2, o_nsa.reshape(b * s, NSA_WIDTH), u_conv, qm, gm, km, vm,
               w_nsa_out, w_conv_out, w_mem_out, w_out, s)
    out = _ffn(h, norm_ffn, w_gate, w_up, w_down)
    return out.reshape(b, s, D_MODEL)


def kernel(x, mem, norm_mix, w_in, nsa_qk_norm, cmp_pos, cmp_w1, cmp_w2, w_nsa_out, conv_w, conv_b, conv_ln_g, conv_ln_b, w_conv_out, norm_mem, w_mem_kv, mem_qk_norm, w_mem_out, w_out, norm_ffn, w_gate, w_up, w_down):
    h = x
    for l in range(norm_mix.shape[0]):
        h = _layer(h, mem, norm_mix[l], w_in[l], nsa_qk_norm[l], cmp_pos[l], cmp_w1[l], cmp_w2[l],
                   w_nsa_out[l], conv_w[l], conv_b[l], conv_ln_g[l], conv_ln_b[l], w_conv_out[l],
                   norm_mem[l], w_mem_kv[l], mem_qk_norm[l], w_mem_out[l], w_out[l], norm_ffn[l],
                   w_gate[l], w_up[l], w_down[l])
    return h
```

```python
import functools

import numpy as np
import jax
import jax.numpy as jnp
from jax import lax
from jax.experimental import pallas as pl
from jax.experimental.pallas import tpu as pltpu

F32 = jnp.float32
BF16 = jnp.bfloat16

D_MODEL = 1024
HEAD_DIM = 64
NSA_HEADS = 8
NSA_GROUPS = 2
NSA_HPG = NSA_HEADS // NSA_GROUPS
NSA_WIDTH = NSA_HEADS * HEAD_DIM
KV_WIDTH = NSA_GROUPS * HEAD_DIM
CMP_BLOCK = 32
CMP_STRIDE = 16
CMP_HIDDEN = 256
SLC_BLOCK = 64
SLC_TOPK = 16
WINDOW = 512
FORCE = 1e6
CONV_WIDTH = 512
CONV_KSIZE = 31
MEM_HEADS = 4
MEM_HEAD_DIM = 128
MEM_WIDTH = MEM_HEADS * MEM_HEAD_DIM
N_BRANCH = 3
D_FF = 2816
RMS_EPS = 1e-6
LN_EPS = 1e-5

LANES = 128
F32_SUBLANES = 8
VMEM_LIMIT = 56 * 1024 * 1024

TM_PROJ = 512
TQ = 512
TK = 1024
WSUB = 128
NSA_CHAINS = 1
LOG2E = 1.4426950408889634
CONV_HALO = 32
CONV_ROWS = 64
TM_MERGE = 1024
TM_FFN = 1024
FF_CHUNK = D_FF // 11
MASKED = -1e30
SEL_BIAS = -1e9
SEL_PAD = TK - TQ
W_ONE = HEAD_DIM + 1


def _dot(a, b):
    return jnp.dot(a, b, preferred_element_type=F32)


def _dot_nt(a, b):
    return lax.dot_general(a, b, (((1,), (1,)), ((), ())), preferred_element_type=F32)


def _head_norm(t, bd, gain):
    ms = _dot((t * t).astype(BF16), bd)
    return t * lax.rsqrt(ms + RMS_EPS) * gain


def _sigmoid(x):
    return 1.0 / (1.0 + jnp.exp(-x))


def _const_spec(shape):
    nd = len(shape)
    return pl.BlockSpec(shape, lambda *_: (0,) * nd, pipeline_mode=pl.Buffered(1))


def _params(*sem):
    return pltpu.CompilerParams(dimension_semantics=sem, vmem_limit_bytes=VMEM_LIMIT)


def _conv_prepare(buf, shifted):
    for rho in range(1, F32_SUBLANES):
        shifted[rho - 1] = buf[rho:rho + shifted.shape[1]]


def _zero_after(t):
    bits = pltpu.bitcast(t, jnp.uint32)
    sixteen = jnp.uint32(16)
    return lax.shift_right_logical(lax.shift_right_logical(bits, sixteen), sixteen).astype(F32)


def _conv_rows(buf, shifted, w_ref, b_ref, g_ref, beta_ref, o_ref, r0, after):
    off = CONV_HALO - (CONV_KSIZE - 1)
    acc = _zero_after(after) + b_ref[...]
    for k in range(CONV_KSIZE):
        rho = (off + k) % F32_SUBLANES
        a = r0 + off + k - rho
        src = buf[a:a + CONV_ROWS] if rho == 0 else shifted[rho - 1, a:a + CONV_ROWS]
        acc = acc + src * w_ref[k:k + 1]
    mu = jnp.mean(acc, axis=-1, keepdims=True)
    d = acc - mu
    var = jnp.mean(d * d, axis=-1, keepdims=True)
    y = d * lax.rsqrt(var + LN_EPS) * g_ref[...] + beta_ref[...]
    o_ref[r0:r0 + CONV_ROWS] = (y * _sigmoid(y)).astype(o_ref.dtype)


def _inproj_kernel(x_ref, g_ref, wq_ref, wkv_ref, wg_ref, wc_ref, wm_ref, wgm_ref,
                   bdq_ref, bdk_ref, bdm_ref, gq_ref, gks_ref, gkw_ref, gqm_ref, cw_ref, cb_ref, lg_ref, lb_ref,
                   q_ref, kc_ref, vc_ref, ks_ref, vs_ref, kw_ref, vw_ref, gn_ref, u_ref, qm_ref, gm_ref,
                   buf, shifted, *, tiles_per_seq):
    tm = x_ref.shape[0]
    n_sel = ks_ref.shape[3] - LANES

    first = pl.program_id(0) % tiles_per_seq == 0

    @pl.when(first)
    def _():
        buf[0:CONV_HALO] = jnp.zeros((CONV_HALO, CONV_WIDTH), F32)

    @pl.when(jnp.logical_not(first))
    def _():
        buf[0:CONV_HALO] = buf[tm:tm + CONV_HALO]

    x = x_ref[...]
    ms = jnp.mean(x * x, axis=-1, keepdims=True)
    xn = (x * lax.rsqrt(ms + RMS_EPS) * g_ref[...]).astype(BF16)

    pc = _dot(xn, wc_ref[...])
    buf[CONV_HALO:] = pc[:, :CONV_WIDTH] * _sigmoid(pc[:, CONV_WIDTH:])
    _conv_prepare(buf, shifted)

    def corner(t):
        return t[0:CONV_ROWS, 0:CONV_WIDTH]

    def seg_q():
        pq = _dot(xn, wq_ref[...])
        qn = _head_norm(pq, bdq_ref[...], gq_ref[...]).astype(BF16)
        for h in range(NSA_HEADS):
            q_ref[0, h] = qn[:, h * HEAD_DIM:(h + 1) * HEAD_DIM]
        return corner(pq)

    def seg_kv():
        pkv = _dot(xn, wkv_ref[...])
        k_c = pkv[:, 0 * KV_WIDTH:1 * KV_WIDTH]
        v_c = pkv[:, 1 * KV_WIDTH:2 * KV_WIDTH]
        k_s = _head_norm(pkv[:, 2 * KV_WIDTH:3 * KV_WIDTH], bdk_ref[...], gks_ref[...]).astype(BF16)
        v_s = pkv[:, 3 * KV_WIDTH:4 * KV_WIDTH].astype(BF16)
        k_w = _head_norm(pkv[:, 4 * KV_WIDTH:5 * KV_WIDTH], bdk_ref[...], gkw_ref[...]).astype(BF16)
        v_w = pkv[:, 5 * KV_WIDTH:6 * KV_WIDTH].astype(BF16)
        gates = _sigmoid(_dot(xn, wg_ref[...])).astype(BF16)

        t0 = (pl.program_id(0) % tiles_per_seq) * tm
        blk_of_row = (t0 + lax.broadcasted_iota(jnp.int32, (tm, n_sel), 0)) // SLC_BLOCK
        onehot = jnp.where(blk_of_row == lax.broadcasted_iota(jnp.int32, (tm, n_sel), 1), 1.0, 0.0).astype(BF16)
        rest = LANES - HEAD_DIM
        zeros = jnp.zeros((tm, rest), BF16)
        rest_lane = lax.broadcasted_iota(jnp.int32, (tm, rest), 1)
        ones_sel = jnp.where(rest_lane == 0, 1.0, 0.0).astype(BF16)
        ones_win = jnp.where(rest_lane == W_ONE - HEAD_DIM, 1.0, 0.0).astype(BF16)
        for g in range(NSA_GROUPS):
            c = slice(g * HEAD_DIM, (g + 1) * HEAD_DIM)
            kc_ref[0, g] = k_c[:, c]
            vc_ref[0, g] = v_c[:, c]
            ks_ref[0, g, :, 0:HEAD_DIM] = k_s[:, c]
            ks_ref[0, g, :, HEAD_DIM:LANES] = zeros
            ks_ref[0, g, :, LANES:] = onehot
            vs_ref[0, g, :, 0:HEAD_DIM] = v_s[:, c]
            vs_ref[0, g, :, HEAD_DIM:] = ones_sel
            kw_ref[0, g, :, 0:HEAD_DIM] = k_w[:, c]
            kw_ref[0, g, :, HEAD_DIM:] = zeros
            vw_ref[0, g, :, 0:HEAD_DIM] = v_w[:, c]
            vw_ref[0, g, :, HEAD_DIM:] = ones_win
            gn_ref[0, g] = gates[:, g * LANES:(g + 1) * LANES]
        return corner(pkv)

    def seg_qm():
        pm = _dot(xn, wm_ref[...])
        qm_ref[...] = _head_norm(pm, bdm_ref[...], gqm_ref[...]).astype(BF16)
        return corner(pm)

    def seg_gm(j):
        c = slice(j * D_MODEL, (j + 1) * D_MODEL)
        pg = _dot(xn, wgm_ref[:, c])
        gm_ref[:, c] = _sigmoid(pg).astype(BF16)
        return corner(pg)

    segments = [seg_q, seg_kv, seg_qm] + [functools.partial(seg_gm, j) for j in range(N_BRANCH)]
    blocks = list(range(0, tm, CONV_ROWS))
    done = 0
    token = segments[0]()
    for j in range(len(segments)):
        nxt = segments[j + 1]() if j + 1 < len(segments) else None
        upto = (j + 1) * len(blocks) // len(segments)
        for r0 in blocks[done:upto]:
            _conv_rows(buf, shifted, cw_ref, cb_ref, lg_ref, lb_ref, u_ref, r0, token)
        done = upto
        token = nxt


def _block_diag(n, blk):
    idx = np.arange(n) // blk
    return jnp.asarray((idx[:, None] == idx[None, :]).astype(np.float32) / blk, dtype=BF16)


def _inproj(x, norm_mix, w_in, nsa_qk_norm, mem_qk_norm, conv_w, conv_b, conv_ln_g, conv_ln_b):
    b, s, _ = x.shape
    m = b * s
    tm = min(TM_PROJ, s)
    per_seq = s // tm
    n_sel = s // SLC_BLOCK
    sizes = [NSA_WIDTH, 6 * KV_WIDTH, 3 * NSA_HEADS, 2 * CONV_WIDTH, MEM_WIDTH, N_BRANCH * D_MODEL]
    pts = np.cumsum(sizes)[:-1]
    wq, wkv, wg, wc, wm, wgm = [w.astype(BF16) for w in jnp.split(w_in, pts, axis=1)]
    per_group = 3 * NSA_HPG
    wg = jnp.pad(wg.reshape(D_MODEL, NSA_GROUPS, per_group), ((0, 0), (0, 0), (0, LANES - per_group)))
    wg = wg.reshape(D_MODEL, NSA_GROUPS * LANES)
    gq = (jnp.tile(nsa_qk_norm[0], NSA_HEADS) * (HEAD_DIM ** -0.5 * LOG2E))[None, :]
    gks = jnp.tile(nsa_qk_norm[2], NSA_GROUPS)[None, :]
    gkw = jnp.tile(nsa_qk_norm[3], NSA_GROUPS)[None, :]
    gqm = (jnp.tile(mem_qk_norm[0], MEM_HEADS) * (MEM_HEAD_DIM ** -0.5))[None, :]
    bdq = _block_diag(NSA_WIDTH, HEAD_DIM)
    bdk = _block_diag(KV_WIDTH, HEAD_DIM)
    bdm = _block_diag(MEM_WIDTH, MEM_HEAD_DIM)
    consts = [norm_mix[None, :], wq, wkv, wg, wc, wm, wgm, bdq, bdk, bdm, gq, gks, gkw, gqm,
              conv_w.reshape(CONV_KSIZE, CONV_WIDTH), conv_b[None, :], conv_ln_g[None, :], conv_ln_b[None, :]]

    def row(n):
        return pl.BlockSpec((tm, n), lambda i: (i, 0))

    def heads(n_heads, width):
        return pl.BlockSpec((1, n_heads, tm, width), lambda i: (i // per_seq, 0, i % per_seq, 0))

    grp = NSA_GROUPS
    outs = [((b, NSA_HEADS, s, HEAD_DIM), BF16, heads(NSA_HEADS, HEAD_DIM)),
            ((b, grp, s, HEAD_DIM), F32, heads(grp, HEAD_DIM)),
            ((b, grp, s, HEAD_DIM), F32, heads(grp, HEAD_DIM)),
            ((b, grp, s, LANES + n_sel), BF16, heads(grp, LANES + n_sel)),
            ((b, grp, s, LANES), BF16, heads(grp, LANES)),
            ((b, grp, s, LANES), BF16, heads(grp, LANES)),
            ((b, grp, s, LANES), BF16, heads(grp, LANES)),
            ((b, grp, s, LANES), BF16, heads(grp, LANES)),
            ((m, CONV_WIDTH), BF16, row(CONV_WIDTH)),
            ((m, MEM_WIDTH), BF16, row(MEM_WIDTH)),
            ((m, N_BRANCH * D_MODEL), BF16, row(N_BRANCH * D_MODEL))]
    return pl.pallas_call(
        functools.partial(_inproj_kernel, tiles_per_seq=per_seq),
        grid=(m // tm,),
        in_specs=[row(D_MODEL)] + [_const_spec(c.shape) for c in consts],
        out_specs=[spec for _, _, spec in outs],
        out_shape=[jax.ShapeDtypeStruct(shape, dt) for shape, dt, _ in outs],
        scratch_shapes=[pltpu.VMEM((CONV_HALO + tm, CONV_WIDTH), F32),
                        pltpu.VMEM((F32_SUBLANES - 1, tm + CONV_HALO - F32_SUBLANES, CONV_WIDTH), F32)],
        compiler_params=_params("arbitrary"),
        name="inproj",
    )(x.reshape(m, D_MODEL), *consts)


def _gelu_tanh(x):
    return 0.5 * x * (1.0 + jnp.tanh(np.sqrt(2.0 / np.pi) * (x + 0.044715 * (x * x * x))))


def _compress_kernel(x_ref, plo_ref, phi_ref, w1a_ref, w1b_ref, w2_ref, gain_ref, o_ref, *, normalize):
    n = x_ref.shape[2] // CMP_STRIDE
    x = jnp.concatenate([x_ref[0, 0, pl.ds(r, n, stride=CMP_STRIDE), :] for r in range(CMP_STRIDE)], axis=1)
    a = _dot((x + plo_ref[...]).astype(BF16), w1a_ref[...])
    b = _dot((x + phi_ref[...]).astype(BF16), w1b_ref[...])
    h = a + pltpu.roll(b, n - 1, 0)
    y = _dot(_gelu_tanh(h).astype(BF16), w2_ref[...])
    if normalize:
        ms = jnp.mean(y * y, axis=-1, keepdims=True)
        y = y * lax.rsqrt(ms + RMS_EPS) * gain_ref[...]
    o_ref[0, 0] = y.astype(o_ref.dtype)


def _compress(xc, pos, w1, w2, gain, normalize):
    b, g, s, width = xc.shape
    n = s // CMP_STRIDE
    half = CMP_STRIDE * HEAD_DIM
    plo = pos[:CMP_STRIDE].reshape(1, half)
    phi = pos[CMP_STRIDE:].reshape(1, half)
    w1a = w1[:half].astype(BF16)
    w1b = w1[half:].astype(BF16)
    consts = [plo, phi, w1a, w1b, w2.astype(BF16), gain[None, :]]
    return pl.pallas_call(
        functools.partial(_compress_kernel, normalize=normalize),
        grid=(b, g),
        in_specs=[pl.BlockSpec((1, 1, s, width), lambda i, j: (i, j, 0, 0))] + [_const_spec(c.shape) for c in consts],
        out_specs=pl.BlockSpec((1, 1, n, HEAD_DIM), lambda i, j: (i, j, 0, 0)),
        out_shape=jax.ShapeDtypeStruct((b, g, n, HEAD_DIM), BF16),
        compiler_params=_params("parallel", "parallel"),
        name="compress_k" if normalize else "compress_v",
    )(xc, *consts)


def _nsa_kernel(q_ref, kc_ref, vc_ref, ks_ref, vs_ref, kw_ref, vw_ref, gate_ref, mt_ref, ex_ref, o_ref,
                qaug_scr, m_scr, acc_scr, *, n_cmp, n_sel):
    tq = q_ref.shape[2]
    hpc = NSA_HPG // NSA_CHAINS
    crow = hpc * tq
    chains = range(NSA_CHAINS)
    i = pl.program_id(2)
    q0 = i * tq

    def q_chain(r):
        return q_ref[0, r * hpc:(r + 1) * hpc].reshape(crow, HEAD_DIM)

    def rows_of(r):
        return slice(r * crow, (r + 1) * crow)

    def stack(a, n=hpc):
        return jnp.concatenate([a] * n, axis=0)

    def causal_bias(n, keep):
        r_i = lax.broadcasted_iota(jnp.int32, (n, n), 0)
        c_i = lax.broadcasted_iota(jnp.int32, (n, n), 1)
        return jnp.where(keep(r_i, c_i), 0.0, MASKED)

    t_col = q0 + lax.broadcasted_iota(jnp.int32, (tq, 1), 0)

    rest = LANES - HEAD_DIM
    flag_col = jnp.where(lax.broadcasted_iota(jnp.int32, (crow, rest), 1) == 0, SEL_BIAS, 0.0).astype(BF16)
    for r in chains:
        qaug_scr[rows_of(r), 0:HEAD_DIM] = q_chain(r)
        qaug_scr[rows_of(r), HEAD_DIM:LANES] = flag_col

    cmp_end = lax.broadcasted_iota(jnp.int32, (1, n_cmp), 1) * CMP_STRIDE + (CMP_BLOCK - 1)
    cbias = stack(jnp.where(cmp_end <= t_col, 0.0, MASKED))
    seen = stack(jnp.where(t_col >= CMP_BLOCK - 1, 1.0, 0.0))
    o_c = []
    psum = None
    for r in chains:
        s = _dot_nt(q_chain(r), kc_ref[0, 0]) + cbias
        e = jnp.exp2(s - jnp.max(s, axis=-1, keepdims=True))
        p = e * (seen / jnp.sum(e, axis=-1, keepdims=True))
        o_c.append(_dot(p.astype(BF16), vc_ref[0, 0]))
        for h in range(hpc):
            ph = p[h * tq:(h + 1) * tq]
            psum = ph if psum is None else psum + ph

    hi = psum.astype(BF16)
    lo = (psum - hi.astype(F32)).astype(BF16)
    imp = _dot_nt(mt_ref[...], hi) + _dot_nt(mt_ref[...], lo)
    blk = lax.broadcasted_iota(jnp.int32, (n_sel, tq), 0)
    t_row = q0 + lax.broadcasted_iota(jnp.int32, (n_sel, tq), 1)
    cur = t_row // SLC_BLOCK
    forced = (blk == 0) | (blk == cur) | (blk == cur - 1)
    future = blk * SLC_BLOCK > t_row
    n_forced = 3
    v = jnp.where(forced, -jnp.inf, jnp.where(future, -FORCE, imp))
    blk_f = blk.astype(F32)

    def extract(v, n):
        for _ in range(n):
            top = jnp.max(v, axis=0, keepdims=True)
            first = jnp.min(jnp.where(v == top, blk_f, float(n_sel)), axis=0, keepdims=True)
            v = jnp.where(blk_f == first, -jnp.inf, v)
        return v

    span = WINDOW + WSUB
    pair = NSA_HPG
    pieces = [(sub, hp) for sub in range(tq // WSUB) for hp in range(NSA_HPG // pair)]
    starts = {sub: pl.multiple_of(q0 + sub * WSUB, WSUB) for sub in range(tq // WSUB)}
    win_scores = {}
    for sub, hp in pieces:
        qw = jnp.concatenate([qaug_scr[h * tq + sub * WSUB:h * tq + (sub + 1) * WSUB, 0:LANES]
                              for h in range(hp * pair, (hp + 1) * pair)], axis=0)
        win_scores[sub, hp] = _dot_nt(qw, kw_ref[0, 0, pl.ds(starts[sub], span), :])
    enter = stack(causal_bias(WSUB, lambda r_i, c_i: c_i > r_i), pair)
    leave = stack(causal_bias(WSUB, lambda r_i, c_i: c_i <= r_i), pair)
    todo = min(SLC_TOPK, n_sel) - n_forced
    per_piece = -(-todo // len(pieces))
    o_w = {}
    for sub, hp in pieces:
        sw = win_scores[sub, hp]
        sw = jnp.concatenate([sw[:, :WSUB] + enter, sw[:, WSUB:WINDOW], sw[:, WINDOW:] + leave], axis=1)
        pw = jnp.exp2(sw - jnp.max(sw, axis=-1, keepdims=True))
        o_w[sub, hp] = _dot(pw.astype(BF16), vw_ref[0, 0, pl.ds(starts[sub], span), :])
        v = extract(v, min(per_piece, todo))
        todo -= min(per_piece, todo)
    bias_t = jnp.where(future, SEL_BIAS, jnp.where(v == -jnp.inf, 0.0, SEL_BIAS))
    bias = stack(jnp.transpose(bias_t).astype(BF16))

    for r in chains:
        qaug_scr[rows_of(r), LANES:LANES + n_sel] = bias
    m_scr[...] = jnp.full(m_scr.shape, MASKED, F32)
    acc_scr[...] = jnp.zeros(acc_scr.shape, F32)

    def sel_chunk(k0, tail_bias):
        for r in chains:
            sc = _dot_nt(qaug_scr[rows_of(r)], ks_ref[0, 0, pl.ds(k0, TK), :])
            if tail_bias is not None:
                sc = jnp.concatenate([sc[:, :TK - tq], sc[:, TK - tq:] + tail_bias], axis=1)
            m_old = m_scr[rows_of(r)]
            m_new = jnp.maximum(m_old, jnp.max(sc, axis=-1, keepdims=True))
            pc = jnp.exp2(sc - jnp.concatenate([m_new] * (TK // LANES), axis=1))
            acc_scr[rows_of(r)] = (jnp.exp2(m_old - m_new) * acc_scr[rows_of(r)]
                                   + _dot(pc.astype(BF16), vs_ref[0, 0, pl.ds(k0, TK), :]))
            m_scr[rows_of(r)] = m_new

    sel_chunk(pl.multiple_of(q0, tq), stack(causal_bias(tq, lambda r_i, c_i: c_i <= r_i)))
    n_low = q0 // TK
    low0 = q0 - n_low * TK

    def low_chunk(c):
        sel_chunk(pl.multiple_of(low0 + c * TK, tq), None)

    def body(c2, carry):
        low_chunk(2 * c2)
        low_chunk(2 * c2 + 1)
        return carry

    lax.fori_loop(0, n_low // 2, body, 0)

    @pl.when(n_low % 2 == 1)
    def _():
        low_chunk(n_low - 1)

    lane = lax.broadcasted_iota(jnp.int32, (tq, LANES), 1)
    gates = gate_ref[0, 0].astype(F32)
    for r in chains:
        acc = acc_scr[rows_of(r)]
        for hh in range(hpc):
            h = r * hpc + hh
            rs = slice(hh * tq, (hh + 1) * tq)
            a_s = acc[rs]
            a_w = jnp.concatenate([o_w[sub, h // pair][(h % pair) * WSUB:(h % pair + 1) * WSUB]
                                   for sub in range(tq // WSUB)], axis=0)
            scal = jnp.where(lane == HEAD_DIM, 1.0 / a_s, jnp.where(lane == W_ONE, 1.0 / a_w, gates))
            y = _dot(scal.astype(BF16), ex_ref[h])

            def slot(k):
                return y[:, k * LANES:k * LANES + HEAD_DIM]

            out = (slot(0) * o_c[r][rs] + (slot(1) * slot(3)) * a_s[:, :HEAD_DIM]
                   + (slot(2) * slot(4)) * a_w[:, :HEAD_DIM])
            o_ref[0, :, h * HEAD_DIM:(h + 1) * HEAD_DIM] = out.astype(o_ref.dtype)


def _importance_matrix(n_cmp, n_sel):
    ratio = SLC_BLOCK // CMP_STRIDE
    c = np.arange(n_cmp)[None, :]
    j = np.arange(n_sel)[:, None]
    return jnp.asarray(((c >= ratio * j - 1) & (c <= ratio * j + ratio - 1)).astype(np.float32), dtype=BF16)


def _spread_matrix():
    e = np.zeros((NSA_HPG, LANES, 5 * LANES), np.float32)
    for h in range(NSA_HPG):
        for k, src in enumerate([3 * h, 3 * h + 1, 3 * h + 2, HEAD_DIM, W_ONE]):
            e[h, src, k * LANES:k * LANES + HEAD_DIM] = 1.0
    return jnp.asarray(e, dtype=BF16)


def _nsa(q, kc, vc, k_aug, vs_aug, kw_aug, vw_aug, gate):
    b, _, s, _ = q.shape
    tq = min(TQ, s)
    n_cmp = s // CMP_STRIDE
    n_sel = s // SLC_BLOCK
    assert n_sel <= LANES and s % TK == 0 and TK % tq == 0 and tq % WSUB == 0 and SEL_PAD == TK - tq

    def left_pad(t, rows, flagged):
        pad = jnp.zeros((rows, t.shape[-1]), BF16)
        if flagged:
            pad = pad.at[:, HEAD_DIM].set(1.0)
        return jnp.concatenate([jnp.broadcast_to(pad, t.shape[:2] + pad.shape), t], axis=2)

    ks_p = left_pad(k_aug, SEL_PAD, True)
    vs_p = left_pad(vs_aug, SEL_PAD, False)
    kw_p = left_pad(kw_aug, WINDOW, True)
    vw_p = left_pad(vw_aug, WINDOW, False)
    mt = _importance_matrix(n_cmp, n_sel)
    ex = _spread_matrix()
    rows = NSA_HPG * tq

    def per_group(shape):
        return pl.BlockSpec((1, 1) + shape, lambda bi, gi, i: (bi, gi, 0, 0))

    return pl.pallas_call(
        functools.partial(_nsa_kernel, n_cmp=n_cmp, n_sel=n_sel),
        grid=(b, NSA_GROUPS, s // tq),
        in_specs=[
            pl.BlockSpec((1, NSA_HPG, tq, HEAD_DIM), lambda bi, gi, i: (bi, gi, i, 0)),
            per_group((n_cmp, HEAD_DIM)), per_group((n_cmp, HEAD_DIM)),
            per_group((s + SEL_PAD, LANES + n_sel)), per_group((s + SEL_PAD, LANES)),
            per_group((s + WINDOW, LANES)), per_group((s + WINDOW, LANES)),
            pl.BlockSpec((1, 1, tq, LANES), lambda bi, gi, i: (bi, gi, i, 0)),
            _const_spec(mt.shape), _const_spec(ex.shape),
        ],
        out_specs=pl.BlockSpec((1, tq, NSA_HPG * HEAD_DIM), lambda bi, gi, i: (bi, i, gi)),
        out_shape=jax.ShapeDtypeStruct((b, s, NSA_WIDTH), BF16),
        scratch_shapes=[pltpu.VMEM((rows, LANES + n_sel), BF16),
                        pltpu.VMEM((rows, LANES), F32),
                        pltpu.VMEM((rows, LANES), F32)],
        compiler_params=_params("parallel", "parallel", "arbitrary"),
        name="nsa",
    )(q, kc, vc, ks_p, vs_p, kw_p, vw_p, gate, mt, ex)


def _memkv_kernel(mem_ref, g_ref, w_ref, bd_ref, gk_ref, km_ref, vm_ref):
    x = mem_ref[0]
    ms = jnp.mean(x * x, axis=-1, keepdims=True)
    xn = (x * lax.rsqrt(ms + RMS_EPS) * g_ref[...]).astype(BF16)
    kv = _dot(xn, w_ref[...])
    km_ref[0] = _head_norm(kv[:, :MEM_WIDTH], bd_ref[...], gk_ref[...]).astype(BF16)
    vm_ref[0] = kv[:, MEM_WIDTH:].astype(BF16)


def _memkv(mem, norm_mem, w_mem_kv, gain_k):
    b, n, _ = mem.shape
    consts = [norm_mem[None, :], w_mem_kv.astype(BF16), _block_diag(MEM_WIDTH, MEM_HEAD_DIM),
              jnp.tile(gain_k, MEM_HEADS)[None, :]]
    spec = pl.BlockSpec((1, n, MEM_WIDTH), lambda i: (i, 0, 0))
    return pl.pallas_call(
        _memkv_kernel,
        grid=(b,),
        in_specs=[pl.BlockSpec((1, n, D_MODEL), lambda i: (i, 0, 0))] + [_const_spec(c.shape) for c in consts],
        out_specs=[spec, spec],
        out_shape=[jax.ShapeDtypeStruct((b, n, MEM_WIDTH), BF16)] * 2,
        compiler_params=_params("parallel"),
        name="memkv",
    )(mem, *consts)


def _merge_kernel(x_ref, onsa_ref, uconv_ref, qm_ref, gm_ref, km_ref, vm_ref,
                  wn_ref, wc_ref, wmo_ref, wo_ref, h_ref):
    qm = qm_ref[...]
    km = km_ref[0]
    vm = vm_ref[0]
    heads = []
    for h in range(MEM_HEADS):
        c = slice(h * MEM_HEAD_DIM, (h + 1) * MEM_HEAD_DIM)
        s = _dot_nt(qm[:, c], km[:, c])
        e = jnp.exp(s - jnp.max(s, axis=-1, keepdims=True))
        p = e * (1.0 / jnp.sum(e, axis=-1, keepdims=True))
        heads.append(_dot(p.astype(BF16), vm[:, c]))
    o_m = jnp.concatenate(heads, axis=1).astype(BF16)
    merged = (gm_ref[:, 0:D_MODEL].astype(F32) * _dot(onsa_ref[...], wn_ref[...])
              + gm_ref[:, D_MODEL:2 * D_MODEL].astype(F32) * _dot(uconv_ref[...], wc_ref[...])
              + gm_ref[:, 2 * D_MODEL:3 * D_MODEL].astype(F32) * _dot(o_m, wmo_ref[...]))
    h_ref[...] = x_ref[...] + _dot(merged.astype(BF16), wo_ref[...])


def _merge(x2, o_nsa, u_conv, qm, gm, km, vm, w_nsa_out, w_conv_out, w_mem_out, w_out, seq):
    m = x2.shape[0]
    tm = min(TM_MERGE, seq)
    per_b = seq // tm
    n_mem = km.shape[1]
    consts = [w_nsa_out.astype(BF16), w_conv_out.astype(BF16), w_mem_out.astype(BF16), w_out.astype(BF16)]

    def row(n):
        return pl.BlockSpec((tm, n), lambda i: (i, 0))

    mem_spec = pl.BlockSpec((1, n_mem, MEM_WIDTH), lambda i: (i // per_b, 0, 0))
    return pl.pallas_call(
        _merge_kernel,
        grid=(m // tm,),
        in_specs=[row(D_MODEL), row(NSA_WIDTH), row(CONV_WIDTH), row(MEM_WIDTH), row(N_BRANCH * D_MODEL),
                  mem_spec, mem_spec] + [_const_spec(c.shape) for c in consts],
        out_specs=row(D_MODEL),
        out_shape=jax.ShapeDtypeStruct((m, D_MODEL), F32),
        compiler_params=_params("parallel"),
        name="merge",
    )(x2, o_nsa, u_conv, qm, gm, km, vm, *consts)


def _ffn_kernel(h_ref, g_ref, wg_ref, wu_ref, wd_ref, o_ref):
    h = h_ref[...]
    ms = jnp.mean(h * h, axis=-1, keepdims=True)
    hn = (h * lax.rsqrt(ms + RMS_EPS) * g_ref[...]).astype(BF16)
    acc = h
    for c in range(0, D_FF, FF_CHUNK):
        gate = _dot(hn, wg_ref[:, c:c + FF_CHUNK])
        up = _dot(hn, wu_ref[:, c:c + FF_CHUNK])
        act = (gate * _sigmoid(gate) * up).astype(BF16)
        acc = acc + _dot(act, wd_ref[c:c + FF_CHUNK, :])
    o_ref[...] = acc


def _ffn(h, norm_ffn, w_gate, w_up, w_down):
    m = h.shape[0]
    tm = min(TM_FFN, m)
    consts = [norm_ffn[None, :], w_gate.astype(BF16), w_up.astype(BF16), w_down.astype(BF16)]
    row = pl.BlockSpec((tm, D_MODEL), lambda i: (i, 0))
    return pl.pallas_call(
        _ffn_kernel,
        grid=(m // tm,),
        in_specs=[row] + [_const_spec(c.shape) for c in consts],
        out_specs=row,
        out_shape=jax.ShapeDtypeStruct((m, D_MODEL), F32),
        compiler_params=_params("parallel"),
        name="ffn",
    )(h, *consts)


def _layer(x, mem, norm_mix, w_in, nsa_qk_norm, cmp_pos, cmp_w1, cmp_w2, w_nsa_out,
           conv_w, conv_b, conv_ln_g, conv_ln_b, w_conv_out, norm_mem, w_mem_kv,
           mem_qk_norm, w_mem_out, w_out, norm_ffn, w_gate, w_up, w_down):
    b, s, _ = x.shape
    x2 = x.reshape(b * s, D_MODEL)
    q, kc_raw, vc_raw, k_aug, vs_aug, kw_aug, vw_aug, gate, u_conv, qm, gm = _inproj(
        x, norm_mix, w_in, nsa_qk_norm, mem_qk_norm, conv_w, conv_b, conv_ln_g, conv_ln_b)

    kc = _compress(kc_raw, cmp_pos[0], cmp_w1[0], cmp_w2[0], nsa_qk_norm[1], True)
    vc = _compress(vc_raw, cmp_pos[1], cmp_w1[1], cmp_w2[1], nsa_qk_norm[1], False)
    o_nsa = _nsa(q, kc, vc, k_aug, vs_aug, kw_aug, vw_aug, gate)

    km, vm = _memkv(mem, norm_mem, w_mem_kv, mem_qk_norm[1])
    h = _merge(x2, o_nsa.reshape(b * s, NSA_WIDTH), u_conv, qm, gm, km, vm,
               w_nsa_out, w_conv_out, w_mem_out, w_out, s)
    out = _ffn(h, norm_ffn, w_gate, w_up, w_down)
    return out.reshape(b, s, D_MODEL)


def kernel(x, mem, norm_mix, w_in, nsa_qk_norm, cmp_pos, cmp_w1, cmp_w2, w_nsa_out, conv_w, conv_b, conv_ln_g, conv_ln_b, w_conv_out, norm_mem, w_mem_kv, mem_qk_norm, w_mem_out, w_out, norm_ffn, w_gate, w_up, w_down):
    h = x
    for l in range(norm_mix.shape[0]):
        h = _layer(h, mem, norm_mix[l], w_in[l], nsa_qk_norm[l], cmp_pos[l], cmp_w1[l], cmp_w2[l],
                   w_nsa_out[l], conv_w[l], conv_b[l], conv_ln_g[l], conv_ln_b[l], w_conv_out[l],
                   norm_mem[l], w_mem_kv[l], mem_qk_norm[l], w_mem_out[l], w_out[l], norm_ffn[l],
                   w_gate[l], w_up[l], w_down[l])
    return h
```
